```python
import math
import jax, jax.numpy as jnp
from jax import lax
import numpy as np

D_MODEL = 4096
BATCH = 2
SEQ = 4096
DEPTH = 2

MIX = D_MODEL
SSM_WIDTH = MIX // 4
SSM_GROUP = 16
SSM_GROUPS = SSM_WIDTH // SSM_GROUP
SSM_STATE = 64
GLA_WIDTH = 3 * MIX // 8
GLA_HEADS = 4
GLA_DV = GLA_WIDTH // GLA_HEADS
GLA_DK = GLA_DV // 2
GLA_QK = GLA_HEADS * GLA_DK
GLA_RANK = 16
GLA_TAU = 16.0
RET_WIDTH = MIX - SSM_WIDTH - GLA_WIDTH
RET_HEADS = 6
RET_DV = RET_WIDTH // RET_HEADS
RET_DK = RET_DV // 2
RET_QK = RET_HEADS * RET_DK
ROPE_BASE = 10000.0
CHUNK = 64
N_EXPERTS = 32
TOP_K = 4
D_EXPERT = 3 * D_MODEL // 16
SWIGLU_LIMIT = 7.0
SWIGLU_ALPHA = 1.702
N_MOD = 6
EPS = 1e-6
IN_SIZES = (SSM_WIDTH, GLA_QK, GLA_QK, GLA_WIDTH, GLA_WIDTH, GLA_RANK, RET_QK, RET_QK, RET_WIDTH, RET_WIDTH)
N_IN = sum(IN_SIZES)

kernel_name = "hybrid_s5_gla_retention_moe_block"

F32 = jnp.float32


def rmsnorm(x, gain):
    xf = x.astype(F32)
    y = xf * lax.rsqrt(jnp.mean(xf * xf, axis=-1, keepdims=True) + EPS)
    return (y * gain.astype(F32)).astype(x.dtype)


def head_rmsnorm(o):
    return o * lax.rsqrt(jnp.mean(o * o, axis=-1, keepdims=True) + EPS)


def modulate(h, shift, scale):
    return h * (1.0 + scale[:, None, :]) + shift[:, None, :]


def split_cols(z):
    parts, off = [], 0
    for n in IN_SIZES:
        parts.append(z[..., off:off + n])
        off += n
    return parts


def to_chunks(t):
    return t.reshape(t.shape[0], t.shape[1] // CHUNK, CHUNK, *t.shape[2:])


def chunk_states(delta, decay):
    def step(s, inp):
        d, a = inp
        return s * a[..., None] + d, s
    s0 = jnp.zeros_like(delta[:, 0])
    _, states = lax.scan(step, s0, (jnp.moveaxis(delta, 1, 0), jnp.moveaxis(decay, 1, 0)))
    return jnp.moveaxis(states, 0, 1)


def rotary(t, positions):
    half = t.shape[-1] // 2
    inv = jnp.power(ROPE_BASE, -jnp.arange(half, dtype=F32) / half)
    ang = positions.astype(F32)[:, :, None] * inv
    cos, sin = jnp.cos(ang)[:, :, None, :], jnp.sin(ang)[:, :, None, :]
    t1, t2 = t[..., :half], t[..., half:]
    return jnp.concatenate([t1 * cos - t2 * sin, t2 * cos + t1 * sin], axis=-1)


def _linear_recurrence(e1, e2):
    a1, b1 = e1
    a2, b2 = e2
    return a1 * a2, a2 * b1 + b2


def s5_mixer(u, lam_re, lam_im, log_step, b_re, b_im, c_re, c_im, d_skip, glu_w, glu_b):
    bsz, s, _ = u.shape
    uf = u.astype(F32).reshape(bsz, s, SSM_GROUPS, SSM_GROUP)
    lam = lax.complex(lam_re.astype(F32), lam_im.astype(F32))
    step = jnp.exp(log_step.astype(F32))[:, None]
    lam_bar = jnp.exp(lam * step)
    b_bar = ((lam_bar - 1.0) / lam)[..., None] * lax.complex(b_re.astype(F32), b_im.astype(F32))
    bu = jnp.einsum('gph,bsgh->bsgp', b_bar, uf.astype(jnp.complex64))
    a = jnp.broadcast_to(lam_bar, bu.shape)
    _, states = lax.associative_scan(_linear_recurrence, (a, bu), axis=1)
    c_mat = lax.complex(c_re.astype(F32), c_im.astype(F32))
    y = jnp.real(jnp.einsum('ghp,bsgp->bsgh', c_mat, states)) + d_skip.astype(F32) * uf
    y = jax.nn.gelu(y.reshape(bsz, s, SSM_WIDTH))
    return y * jax.nn.sigmoid(y @ glu_w.astype(F32) + glu_b.astype(F32))


def gla_mixer(q, k, v, g, za, wa, ba, norm_gain):
    bsz, s, _ = q.shape
    q = q.astype(F32).reshape(bsz, s, GLA_HEADS, GLA_DK) * (GLA_DK ** -0.5)
    k = k.astype(F32).reshape(bsz, s, GLA_HEADS, GLA_DK)
    v = v.astype(F32).reshape(bsz, s, GLA_HEADS, GLA_DV)
    log_a = jax.nn.log_sigmoid(za.astype(F32) @ wa.astype(F32) + ba.astype(F32)) / GLA_TAU
    log_a = log_a.reshape(bsz, s, GLA_HEADS, GLA_DK)
    q, k, v, log_a = to_chunks(q), to_chunks(k), to_chunks(v), to_chunks(log_a)
    b = jnp.cumsum(log_a, axis=2)
    b_last = b[:, :, -1:]
    q_dec = q * jnp.exp(b)
    k_intra = k * jnp.exp(-b)
    k_state = k * jnp.exp(b_last - b)
    causal = jnp.tril(jnp.ones((CHUNK, CHUNK), dtype=bool))
    scores = jnp.where(causal, jnp.einsum('bnihd,bnjhd->bnhij', q_dec, k_intra), 0.0)
    o_intra = jnp.einsum('bnhij,bnjhe->bnihe', scores, v)
    delta = jnp.einsum('bnjhd,bnjhe->bnhde', k_state, v)
    states = chunk_states(delta, jnp.exp(b_last[:, :, 0]))
    o_inter = jnp.einsum('bnihd,bnhde->bnihe', q_dec, states)
    o = head_rmsnorm((o_intra + o_inter).reshape(bsz, s, GLA_HEADS, GLA_DV))
    return o.reshape(bsz, s, GLA_WIDTH) * norm_gain.astype(F32) * jax.nn.silu(g.astype(F32))


def retention_mixer(q, k, v, g, positions, norm_gain):
    bsz, s, _ = q.shape
    n_chunks = s // CHUNK
    q = rotary(q.astype(F32).reshape(bsz, s, RET_HEADS, RET_DK), positions) * (RET_DK ** -0.5)
    k = rotary(k.astype(F32).reshape(bsz, s, RET_HEADS, RET_DK), positions)
    v = v.astype(F32).reshape(bsz, s, RET_HEADS, RET_DV)
    q, k, v = to_chunks(q), to_chunks(k), to_chunks(v)
    log_gamma = jnp.log1p(-jnp.power(2.0, -5.0 - jnp.arange(RET_HEADS, dtype=F32)))
    idx = jnp.arange(CHUNK, dtype=F32)
    rel = idx[:, None] - idx[None, :]
    decay_mask = jnp.where(rel >= 0, jnp.exp(jnp.maximum(rel, 0.0)[None] * log_gamma[:, None, None]), 0.0)
    scores = jnp.einsum('bnihd,bnjhd->bnhij', q, k) * decay_mask
    o_intra = jnp.einsum('bnhij,bnjhe->bnihe', scores, v)
    k_w = jnp.exp((CHUNK - 1.0 - idx)[:, None] * log_gamma)
    delta = jnp.einsum('bnjhd,bnjhe->bnhde', k * k_w[:, :, None], v)
    chunk_decay = jnp.broadcast_to(jnp.exp(CHUNK * log_gamma)[:, None], (bsz, n_chunks, RET_HEADS, RET_DK))
    states = chunk_states(delta, chunk_decay)
    q_w = jnp.exp((idx + 1.0)[:, None] * log_gamma)
    o_inter = jnp.einsum('bnihd,bnhde->bnihe', q * q_w[:, :, None], states)
    o = head_rmsnorm((o_intra + o_inter).reshape(bsz, s, RET_HEADS, RET_DV))
    return o.reshape(bsz, s, RET_WIDTH) * norm_gain.astype(F32) * jax.nn.silu(g.astype(F32))


def moe_ffn(h, router_w, router_b, w_gu, b_gu, w_down, b_down):
    bsz, s, d = h.shape
    t = h.reshape(bsz * s, d)
    logits = (t @ router_w + router_b).astype(F32)
    top_val, top_idx = lax.top_k(logits, TOP_K)
    top_w = jax.nn.softmax(top_val, axis=-1)
    combine = jnp.sum(jax.nn.one_hot(top_idx, N_EXPERTS, dtype=F32) * top_w[..., None], axis=1)
    out = jnp.zeros((bsz * s, d), F32)
    for e in range(N_EXPERTS):
        gu = (t @ w_gu[e] + b_gu[e]).astype(F32)
        gate = jnp.minimum(gu[:, :D_EXPERT], SWIGLU_LIMIT)
        up = jnp.clip(gu[:, D_EXPERT:], -SWIGLU_LIMIT, SWIGLU_LIMIT)
        act = (up + 1.0) * gate * jax.nn.sigmoid(SWIGLU_ALPHA * gate)
        y = act.astype(h.dtype) @ w_down[e] + b_down[e]
        out = out + combine[:, e:e + 1] * y.astype(F32)
    return out.reshape(bsz, s, d).astype(h.dtype)


def setup_inputs(seed: int = 0) -> dict:
    key = jax.random.key(seed)
    kit = iter(list(jax.random.split(key, 40)))

    def nrm(shape, scale):
        return scale * jax.random.normal(next(kit), shape, F32)

    L, D = DEPTH, D_MODEL
    x = nrm((BATCH, SEQ, D), 1.0)
    c = nrm((BATCH, D), 1.0)
    positions = (jax.random.randint(next(kit), (BATCH, 1), 0, 1024, dtype=jnp.int32)
                 + jnp.arange(SEQ, dtype=jnp.int32)[None, :])
    ada_w = nrm((L, D, N_MOD * D), 0.5 * D ** -0.5)
    ada_b = nrm((L, N_MOD * D), 0.01)
    mix_pre_gain = 1.0 + nrm((L, D), 0.05)
    mix_post_gain = 1.0 + nrm((L, D), 0.05)
    ffn_pre_gain = 1.0 + nrm((L, D), 0.05)
    ffn_post_gain = 1.0 + nrm((L, D), 0.05)
    w_in = nrm((L, D, N_IN), D ** -0.5)
    w_out = nrm((L, MIX, D), MIX ** -0.5)
    ssm_lam_re = -0.5 * jnp.exp(nrm((L, SSM_GROUPS, SSM_STATE), 0.05))
    ssm_lam_im = jnp.pi * jnp.arange(SSM_STATE, dtype=F32) + nrm((L, SSM_GROUPS, SSM_STATE), 0.05)
    ssm_log_step = jax.random.uniform(next(kit), (L, SSM_GROUPS), F32, math.log(1e-3), math.log(1e-1))
    ssm_b_re = nrm((L, SSM_GROUPS, SSM_STATE, SSM_GROUP), (2 * SSM_GROUP) ** -0.5)
    ssm_b_im = nrm((L, SSM_GROUPS, SSM_STATE, SSM_GROUP), (2 * SSM_GROUP) ** -0.5)
    ssm_c_re = nrm((L, SSM_GROUPS, SSM_GROUP, SSM_STATE), (2 * SSM_STATE) ** -0.5)
    ssm_c_im = nrm((L, SSM_GROUPS, SSM_GROUP, SSM_STATE), (2 * SSM_STATE) ** -0.5)
    ssm_d = nrm((L, SSM_GROUPS, SSM_GROUP), 1.0)
    ssm_glu_w = nrm((L, SSM_WIDTH, SSM_WIDTH), SSM_WIDTH ** -0.5)
    ssm_glu_b = nrm((L, SSM_WIDTH), 0.01)
    gla_wa = nrm((L, GLA_RANK, GLA_QK), GLA_RANK ** -0.5)
    gla_ba = nrm((L, GLA_QK), 0.1)
    gla_norm_gain = 1.0 + nrm((L, GLA_WIDTH), 0.05)
    ret_norm_gain = 1.0 + nrm((L, RET_WIDTH), 0.05)
    router_w = nrm((L, D, N_EXPERTS), D ** -0.5)
    router_b = nrm((L, N_EXPERTS), 0.01)
    exp_w_gu = nrm((L, N_EXPERTS, D, 2 * D_EXPERT), D ** -0.5)
    exp_b_gu = nrm((L, N_EXPERTS, 2 * D_EXPERT), 0.01)
    exp_w_down = nrm((L, N_EXPERTS, D_EXPERT, D), D_EXPERT ** -0.5)
    exp_b_down = nrm((L, N_EXPERTS, D), 0.01)
    return {"x": x, "c": c, "positions": positions, "ada_w": ada_w, "ada_b": ada_b,
            "mix_pre_gain": mix_pre_gain, "mix_post_gain": mix_post_gain,
            "ffn_pre_gain": ffn_pre_gain, "ffn_post_gain": ffn_post_gain,
            "w_in": w_in, "w_out": w_out,
            "ssm_lam_re": ssm_lam_re, "ssm_lam_im": ssm_lam_im, "ssm_log_step": ssm_log_step,
            "ssm_b_re": ssm_b_re, "ssm_b_im": ssm_b_im, "ssm_c_re": ssm_c_re, "ssm_c_im": ssm_c_im,
            "ssm_d": ssm_d, "ssm_glu_w": ssm_glu_w, "ssm_glu_b": ssm_glu_b,
            "gla_wa": gla_wa, "gla_ba": gla_ba, "gla_norm_gain": gla_norm_gain,
            "ret_norm_gain": ret_norm_gain, "router_w": router_w, "router_b": router_b,
            "exp_w_gu": exp_w_gu, "exp_b_gu": exp_b_gu, "exp_w_down": exp_w_down, "exp_b_down": exp_b_down}


def reference(x, c, positions, ada_w, ada_b, mix_pre_gain, mix_post_gain, ffn_pre_gain, ffn_post_gain,
              w_in, w_out, ssm_lam_re, ssm_lam_im, ssm_log_step, ssm_b_re, ssm_b_im, ssm_c_re, ssm_c_im,
              ssm_d, ssm_glu_w, ssm_glu_b, gla_wa, gla_ba, gla_norm_gain, ret_norm_gain,
              router_w, router_b, exp_w_gu, exp_b_gu, exp_w_down, exp_b_down):
    c_act = jax.nn.silu(c)
    for l in range(DEPTH):
        mod = c_act @ ada_w[l] + ada_b[l]
        shift1, scale1, gate1, shift2, scale2, gate2 = jnp.split(mod, N_MOD, axis=-1)
        h = modulate(rmsnorm(x, mix_pre_gain[l]), shift1, scale1)
        z = h @ w_in[l]
        u, gq, gk, gv, gg, ga, rq, rk, rv, rg = split_cols(z)
        y_ssm = s5_mixer(u, ssm_lam_re[l], ssm_lam_im[l], ssm_log_step[l], ssm_b_re[l], ssm_b_im[l],
                         ssm_c_re[l], ssm_c_im[l], ssm_d[l], ssm_glu_w[l], ssm_glu_b[l])
        y_gla = gla_mixer(gq, gk, gv, gg, ga, gla_wa[l], gla_ba[l], gla_norm_gain[l])
        y_ret = retention_mixer(rq, rk, rv, rg, positions, ret_norm_gain[l])
        y = jnp.concatenate([y_ssm, y_gla, y_ret], axis=-1).astype(x.dtype) @ w_out[l]
        x = x + gate1[:, None, :] * rmsnorm(y, mix_post_gain[l])
        h = modulate(rmsnorm(x, ffn_pre_gain[l]), shift2, scale2)
        y = moe_ffn(h, router_w[l], router_b[l], exp_w_gu[l], exp_b_gu[l], exp_w_down[l], exp_b_down[l])
        x = x + gate2[:, None, :] * rmsnorm(y, ffn_post_gain[l])
    return x
```

```python
import functools
import math

import jax
import jax.numpy as jnp
from jax import lax
from jax.experimental import pallas as pl
from jax.experimental.pallas import tpu as pltpu

F32 = jnp.float32
BF16 = jnp.bfloat16
I32 = jnp.int32

D_MODEL = 4096
N_MOD = 6
SSM_WIDTH = 1024
SSM_GROUP = 16
SSM_GROUPS = 64
SSM_STATE = 64
SSM_LANES = SSM_GROUPS * SSM_STATE
GLA_WIDTH = 1536
GLA_HEADS = 4
GLA_DV = 384
GLA_DK = 192
GLA_QK = 768
GLA_RANK = 16
GLA_TAU = 16.0
GLA_CHUNK = 64
RET_WIDTH = 1536
RET_HEADS = 6
RET_DV = 256
RET_DK = 128
RET_QK = 768
ROPE_BASE = 10000.0
N_EXPERTS = 32
TOP_K = 4
D_EXPERT = 768
SWIGLU_LIMIT = 7.0
SWIGLU_ALPHA = 1.702
EPS = 1e-6
OFF_U = 0
OFF_GQK = 1024
OFF_GV = 2560
OFF_GG = 4096
OFF_GA = 5632
OFF_TAIL = 5648

LANES = 128
SUBLANES = 8
VMEM_LIMIT = 52 * 1024 * 1024

TM_ROW = 256
TM_MM = 512
TN_MM = 512
TB_SSM = 256
SSM_LC = 256
SSM_GB = 16
TB_GLA = 256
TB_RET = 256
TM_MOE = 256
TN_GU = 384
TN_DOWN = 2048
TM_CMB = 128
GATHER_ROWS = 4096
HALF_D = D_MODEL // 2


def _cparams(sem):
    return pltpu.CompilerParams(dimension_semantics=sem, vmem_limit_bytes=VMEM_LIMIT)


def _dot(a, b):
    return jnp.dot(a, b, preferred_element_type=F32)


def _dot_nt(a, b):
    return lax.dot_general(a, b, (((1,), (1,)), ((), ())), preferred_element_type=F32)


def _dot_tn(a, b):
    return lax.dot_general(a, b, (((0,), (0,)), ((), ())), preferred_element_type=F32)


def _split_bf16(x):
    hi = x.astype(BF16)
    lo = (x - hi.astype(F32)).astype(BF16)
    return hi, lo


def _dot3(a, b):
    ah, al = _split_bf16(a)
    bh, bl = _split_bf16(b)
    return _dot(ah, bh) + _dot(al, bh) + _dot(ah, bl)


def _ada_kernel(c_ref, w_ref, b_ref, o_ref):
    nb = c_ref.shape[0]
    d, tn = w_ref.shape
    rk = 64

    def body(i, accs):
        r0 = pl.multiple_of(i * rk, rk)
        wk = w_ref[pl.ds(r0, rk), :]
        out = []
        for b in range(nb):
            cc = c_ref[b, pl.ds(r0, rk), :]
            ca = cc * jax.nn.sigmoid(cc)
            out.append(accs[b] + jnp.sum((wk * ca).reshape(rk // SUBLANES, SUBLANES, tn), axis=0))
        return tuple(out)

    accs = lax.fori_loop(0, d // rk, body, tuple(jnp.zeros((SUBLANES, tn), F32) for _ in range(nb)))
    for b in range(nb):
        o_ref[b:b + 1, :] = jnp.sum(accs[b], axis=0, keepdims=True) + b_ref[...]


def _ada(c, ada_w, ada_b):
    n_layers, d, n = ada_w.shape
    nb = c.shape[0]
    tn = 512
    return pl.pallas_call(
        _ada_kernel,
        out_shape=jax.ShapeDtypeStruct((n_layers, nb, n), F32),
        grid=(n_layers, n // tn),
        in_specs=[
            pl.BlockSpec((nb, d, 1), lambda l, j: (0, 0, 0)),
            pl.BlockSpec((None, d, tn), lambda l, j: (l, 0, j)),
            pl.BlockSpec((None, 1, tn), lambda l, j: (l, 0, j)),
        ],
        out_specs=pl.BlockSpec((None, nb, tn), lambda l, j: (l, 0, j)),
        compiler_params=_cparams(("arbitrary", "arbitrary")),
        name="ada_mod",
    )(c.reshape(nb, d, 1), ada_w, ada_b.reshape(n_layers, 1, n))


def _rms(x):
    return x * lax.rsqrt(jnp.mean(x * x, axis=-1, keepdims=True) + EPS)


def _prenorm_kernel(x_ref, g_ref, mod_ref, o_ref, *, shift_i, scale_i):
    m = mod_ref[...]
    h = _rms(x_ref[...]) * g_ref[...]
    o_ref[...] = (h * (1.0 + m[scale_i:scale_i + 1]) + m[shift_i:shift_i + 1]).astype(o_ref.dtype)


def _prenorm(x2, gain, mod6, seq, *, shift_i, scale_i):
    t, d = x2.shape
    per_b = seq // TM_ROW
    return pl.pallas_call(
        functools.partial(_prenorm_kernel, shift_i=shift_i, scale_i=scale_i),
        out_shape=jax.ShapeDtypeStruct((t, d), BF16),
        grid=(t // TM_ROW,),
        in_specs=[
            pl.BlockSpec((TM_ROW, d), lambda i: (i, 0)),
            pl.BlockSpec((1, d), lambda i: (0, 0)),
            pl.BlockSpec((None, N_MOD, d), lambda i: (i // per_b, 0, 0)),
        ],
        out_specs=pl.BlockSpec((TM_ROW, d), lambda i: (i, 0)),
        compiler_params=_cparams(("arbitrary",)),
        name="prenorm",
    )(x2, gain.reshape(1, d), mod6)


def _mm_kernel(x_ref, w_ref, o_ref, wbf_ref):
    @pl.when(pl.program_id(1) == 0)
    def _():
        wbf_ref[...] = w_ref[...].astype(BF16)

    o_ref[...] = _dot(x_ref[...], wbf_ref[...]).astype(o_ref.dtype)


def _mm(x, w_full, layer, col_off, width, *, tn=TN_MM, out_dtype=F32):
    t, k = x.shape
    jb = col_off // tn
    return pl.pallas_call(
        _mm_kernel,
        out_shape=jax.ShapeDtypeStruct((t, width), out_dtype),
        grid=(width // tn, t // TM_MM),
        in_specs=[
            pl.BlockSpec((TM_MM, k), lambda j, i: (i, 0)),
            pl.BlockSpec((None, k, tn), lambda j, i: (layer, 0, jb + j)),
        ],
        out_specs=pl.BlockSpec((TM_MM, tn), lambda j, i: (i, j)),
        scratch_shapes=[pltpu.VMEM((k, tn), BF16)],
        compiler_params=_cparams(("arbitrary", "arbitrary")),
        name="proj_in",
    )(x, w_full)


def _mm3_kernel(a_ref, b_ref, c_ref, w_ref, o_ref, wbf_ref):
    @pl.when(pl.program_id(1) == 0)
    def _():
        wbf_ref[...] = w_ref[...].astype(BF16)

    ka, kb = a_ref.shape[1], b_ref.shape[1]
    acc = _dot(a_ref[...], wbf_ref[0:ka, :])
    acc += _dot(b_ref[...], wbf_ref[ka:ka + kb, :])
    acc += _dot(c_ref[...], wbf_ref[ka + kb:, :])
    o_ref[...] = acc


def _mm_out(ya, yb, yc, w_full, layer):
    t = ya.shape[0]
    k, n = w_full.shape[1], w_full.shape[2]
    return pl.pallas_call(
        _mm3_kernel,
        out_shape=jax.ShapeDtypeStruct((t, n), F32),
        grid=(n // TN_MM, t // TM_MM),
        in_specs=[
            pl.BlockSpec((TM_MM, ya.shape[1]), lambda j, i: (i, 0)),
            pl.BlockSpec((TM_MM, yb.shape[1]), lambda j, i: (i, 0)),
            pl.BlockSpec((TM_MM, yc.shape[1]), lambda j, i: (i, 0)),
            pl.BlockSpec((None, k, TN_MM), lambda j, i: (layer, 0, j)),
        ],
        out_specs=pl.BlockSpec((TM_MM, TN_MM), lambda j, i: (i, j)),
        scratch_shapes=[pltpu.VMEM((k, TN_MM), BF16)],
        compiler_params=_cparams(("arbitrary", "arbitrary")),
        name="proj_out",
    )(ya, yb, yc, w_full)


def _ssm_params(lam_re, lam_im, log_step, b_re, b_im, c_re, c_im):
    step = jnp.exp(log_step)[:, None]
    ar, ai = lam_re * step, lam_im * step
    mag = jnp.exp(ar)
    lbr, lbi = mag * jnp.cos(ai), mag * jnp.sin(ai)
    den = lam_re * lam_re + lam_im * lam_im
    cfr = ((lbr - 1.0) * lam_re + lbi * lam_im) / den
    cfi = (lbi * lam_re - (lbr - 1.0) * lam_im) / den
    bbr = cfr[..., None] * b_re - cfi[..., None] * b_im
    bbi = cfr[..., None] * b_im + cfi[..., None] * b_re
    nj = SSM_GROUPS // SSM_GB
    eye = jnp.eye(SSM_GB, dtype=F32)

    def bmat(bb):
        bb = bb.reshape(nj, SSM_GB, SSM_STATE, SSM_GROUP)
        m = jnp.einsum("ab,jbph->jahbp", eye, bb)
        return m.reshape(nj, SSM_GB * SSM_GROUP, SSM_GB * SSM_STATE)

    def cmat(cc):
        cc = cc.reshape(nj, SSM_GB, SSM_GROUP, SSM_STATE)
        m = jnp.einsum("ab,jbhp->japbh", eye, cc)
        return m.reshape(nj, SSM_GB * SSM_STATE, SSM_GB * SSM_GROUP)

    b_mat = jnp.concatenate([bmat(bbr), bmat(bbi)], axis=-1).astype(BF16)
    c_mat = jnp.concatenate([cmat(c_re), -cmat(c_im)], axis=1).astype(BF16)

    def power(kk):
        m = jnp.exp(kk * ar[None])
        return ((m * jnp.cos(kk * ai[None])).reshape(-1, SSM_LANES),
                (m * jnp.sin(kk * ai[None])).reshape(-1, SSM_LANES))

    rows = jnp.arange(SUBLANES, dtype=F32)[:, None]
    shifts = []
    for s in (1, 2, 4):
        pr, pi = power(jnp.full((1, 1, 1), float(s), F32))
        keep = rows >= s
        shifts += [jnp.where(keep, pr, 0.0), jnp.where(keep, pi, 0.0)]
    pr, pi = power(jnp.arange(1, SUBLANES + 1, dtype=F32)[:, None, None])
    lam_tab = jnp.stack(shifts + [pr, pi])
    return b_mat, c_mat, lam_tab


def _ssm_kernel(u_ref, bm_ref, cm_ref, lt_ref, d_ref, gw_ref, gb_ref, o_ref, xr_ref, xi_ref, st_ref):
    tb = u_ref.shape[0]
    nj = bm_ref.shape[0]
    gw = SSM_GB * SSM_GROUP
    sw = SSM_GB * SSM_STATE

    @pl.when(pl.program_id(1) == 0)
    def _():
        st_ref[...] = jnp.zeros_like(st_ref)

    u = u_ref[...]
    ub = u.astype(BF16)
    for j in range(nj):
        bu = _dot(ub[:, j * gw:(j + 1) * gw], bm_ref[j])
        xr_ref[:, j * sw:(j + 1) * sw] = bu[:, :sw]
        xi_ref[:, j * sw:(j + 1) * sw] = bu[:, sw:]

    for c in range(SSM_LANES // SSM_LC):
        sl = slice(c * SSM_LC, (c + 1) * SSM_LC)
        tabs = [lt_ref[k, :, sl] for k in range(8)]

        def row_body(r, carry, sl=sl, tabs=tabs):
            cr, ci = carry
            rs = pl.ds(pl.multiple_of(r * SUBLANES, SUBLANES), SUBLANES)
            xr = xr_ref[rs, sl]
            xi = xi_ref[rs, sl]
            for k, s in enumerate((1, 2, 4)):
                lr, li = tabs[2 * k], tabs[2 * k + 1]
                sr = pltpu.roll(xr, s, axis=0)
                si = pltpu.roll(xi, s, axis=0)
                xr, xi = xr + (lr * sr - li * si), xi + (lr * si + li * sr)
            pr, pi = tabs[6], tabs[7]
            xr, xi = xr + (pr * cr - pi * ci), xi + (pr * ci + pi * cr)
            xr_ref[rs, sl] = xr
            xi_ref[rs, sl] = xi
            return xr[SUBLANES - 1:, :], xi[SUBLANES - 1:, :]

        cr, ci = lax.fori_loop(0, tb // SUBLANES, row_body, (st_ref[0:1, sl], st_ref[1:2, sl]), unroll=2)
        st_ref[0:1, sl] = cr
        st_ref[1:2, sl] = ci

    ys = []
    for j in range(nj):
        xs = jnp.concatenate([xr_ref[:, j * sw:(j + 1) * sw], xi_ref[:, j * sw:(j + 1) * sw]], axis=-1)
        ys.append(_dot(xs.astype(BF16), cm_ref[j]))
    y = jnp.concatenate(ys, axis=-1) + d_ref[...] * u
    y = jax.nn.gelu(y)
    gate = _dot(y.astype(BF16), gw_ref[...]) + gb_ref[...]
    o_ref[...] = (y * jax.nn.sigmoid(gate)).astype(o_ref.dtype)


def _ssm(u, b_mat, c_mat, lam_tab, d_skip, glu_w, glu_b, nbatch):
    t, w = u.shape
    per_b = t // nbatch // TB_SSM
    full = lambda a: pl.BlockSpec(a.shape, lambda b, i: (0,) * a.ndim)
    d_row = d_skip.reshape(1, w)
    gb_row = glu_b.reshape(1, w)
    return pl.pallas_call(
        _ssm_kernel,
        out_shape=jax.ShapeDtypeStruct((t, w), BF16),
        grid=(nbatch, per_b),
        in_specs=[
            pl.BlockSpec((TB_SSM, w), lambda b, i: (b * per_b + i, 0)),
            full(b_mat), full(c_mat), full(lam_tab), full(d_row), full(glu_w), full(gb_row),
        ],
        out_specs=pl.BlockSpec((TB_SSM, w), lambda b, i: (b * per_b + i, 0)),
        scratch_shapes=[
            pltpu.VMEM((TB_SSM, SSM_LANES), F32),
            pltpu.VMEM((TB_SSM, SSM_LANES), F32),
            pltpu.VMEM((2, SSM_LANES), F32),
        ],
        compiler_params=_cparams(("arbitrary", "arbitrary")),
        name="s5_mixer",
    )(u, b_mat, c_mat, lam_tab, d_row, glu_w, gb_row)


def _log_sigmoid(z):
    return jnp.minimum(z, 0.0) - jnp.log1p(jnp.exp(-jnp.abs(z)))


def _silu(x):
    return x * jax.nn.sigmoid(x)


def _gla_kernel(qk_ref, v_ref, g_ref, a_ref, wa_ref, ba_ref, gain_ref, o_ref, st_ref):
    tb = qk_ref.shape[0]
    ck = GLA_CHUNK

    @pl.when(pl.program_id(1) == 0)
    def _():
        st_ref[...] = jnp.zeros_like(st_ref)

    log_a = _log_sigmoid(_dot3(a_ref[:, :GLA_RANK], wa_ref[...]) + ba_ref[...]) * (1.0 / GLA_TAU)
    ri = lax.broadcasted_iota(I32, (ck, ck), 0)
    ci = lax.broadcasted_iota(I32, (ck, ck), 1)
    causal = ri >= ci
    tri = jnp.where(causal, 1.0, 0.0).astype(BF16)
    for c in range(tb // ck):
        rows = slice(c * ck, (c + 1) * ck)
        la_hi, la_lo = _split_bf16(log_a[rows])
        dec = _dot(tri, la_hi) + _dot(tri, la_lo)
        for h in range(GLA_HEADS):
            kcol = slice(h * GLA_DK, (h + 1) * GLA_DK)
            vcol = slice(h * GLA_DV, (h + 1) * GLA_DV)
            b = dec[:, kcol]
            b_last = b[ck - 1:ck, :]
            q = qk_ref[rows, kcol] * (GLA_DK ** -0.5)
            k = qk_ref[rows, GLA_QK + h * GLA_DK:GLA_QK + (h + 1) * GLA_DK]
            v = v_ref[rows, vcol].astype(BF16)
            q_dec = (q * jnp.exp(b)).astype(BF16)
            k_intra = (k * jnp.exp(-b)).astype(BF16)
            k_state = (k * jnp.exp(b_last - b)).astype(BF16)
            scores = jnp.where(causal, _dot_nt(q_dec, k_intra), 0.0)
            st = st_ref[h]
            o = _dot(scores.astype(BF16), v) + _dot_nt(q_dec, st.astype(BF16))
            st_ref[h] = st * jnp.exp(b_last) + _dot_tn(v, k_state)
            o = _rms(o) * gain_ref[:, vcol] * _silu(g_ref[rows, vcol])
            o_ref[rows, vcol] = o.astype(o_ref.dtype)


def _gla(gqk, gv, gg, ga, wa, ba, gain, nbatch):
    t = gqk.shape[0]
    per_b = t // nbatch // TB_GLA
    row = lambda w: pl.BlockSpec((TB_GLA, w), lambda b, i: (b * per_b + i, 0))
    full = lambda a: pl.BlockSpec(a.shape, lambda b, i: (0,) * a.ndim)
    ba_row = ba.reshape(1, GLA_QK)
    gain_row = gain.reshape(1, GLA_WIDTH)
    return pl.pallas_call(
        _gla_kernel,
        out_shape=jax.ShapeDtypeStruct((t, GLA_WIDTH), BF16),
        grid=(nbatch, per_b),
        in_specs=[row(2 * GLA_QK), row(GLA_WIDTH), row(GLA_WIDTH), row(LANES),
                  full(wa), full(ba_row), full(gain_row)],
        out_specs=row(GLA_WIDTH),
        scratch_shapes=[pltpu.VMEM((GLA_HEADS, GLA_DV, GLA_DK), F32)],
        compiler_params=_cparams(("arbitrary", "arbitrary")),
        name="gla_mixer",
    )(gqk, gv, gg, ga, wa, ba_row, gain_row)


def _ret_kernel(qk_ref, v_ref, g_ref, pos_ref, inv_ref, gain_ref, o_ref, st_ref):
    tb = qk_ref.shape[0]

    @pl.when(pl.program_id(1) == 0)
    def _():
        st_ref[...] = jnp.zeros_like(st_ref)

    ang = pos_ref[...].astype(F32) * inv_ref[...]
    lane = lax.broadcasted_iota(I32, (1, RET_DK), 1)
    cos = jnp.cos(ang)
    sin = jnp.sin(ang) * jnp.where(lane < RET_DK // 2, -1.0, 1.0)
    ri = lax.broadcasted_iota(I32, (tb, tb), 0)
    ci = lax.broadcasted_iota(I32, (tb, tb), 1)
    rel = (ri - ci).astype(F32)
    idx = lax.broadcasted_iota(I32, (tb, 1), 0).astype(F32)
    for h in range(RET_HEADS):
        log_gamma = math.log1p(-(2.0 ** (-5.0 - h)))
        kcol = slice(h * RET_DK, (h + 1) * RET_DK)
        vcol = slice(h * RET_DV, (h + 1) * RET_DV)
        q = qk_ref[:, kcol]
        k = qk_ref[:, RET_QK + h * RET_DK:RET_QK + (h + 1) * RET_DK]
        q = (q * cos + pltpu.roll(q, RET_DK // 2, axis=1) * sin) * (RET_DK ** -0.5)
        k = k * cos + pltpu.roll(k, RET_DK // 2, axis=1) * sin
        v = v_ref[:, vcol].astype(BF16)
        decay = jnp.where(rel >= 0, jnp.exp(jnp.maximum(rel, 0.0) * log_gamma), 0.0)
        scores = _dot_nt(q.astype(BF16), k.astype(BF16)) * decay
        q_w = jnp.exp((idx + 1.0) * log_gamma)
        k_w = jnp.exp((tb - 1.0 - idx) * log_gamma)
        st = st_ref[h]
        o = _dot(scores.astype(BF16), v) + _dot((q * q_w).astype(BF16), st.astype(BF16))
        st_ref[h] = st * math.exp(tb * log_gamma) + _dot_tn((k * k_w).astype(BF16), v)
        o = _rms(o) * gain_ref[:, vcol] * _silu(g_ref[:, vcol])
        o_ref[:, vcol] = o.astype(o_ref.dtype)


def _ret(rqk, rv, rg, pos, inv, gain, nbatch):
    t = rqk.shape[0]
    per_b = t // nbatch // TB_RET
    row = lambda w: pl.BlockSpec((TB_RET, w), lambda b, i: (b * per_b + i, 0))
    full = lambda a: pl.BlockSpec(a.shape, lambda b, i: (0,) * a.ndim)
    gain_row = gain.reshape(1, RET_WIDTH)
    return pl.pallas_call(
        _ret_kernel,
        out_shape=jax.ShapeDtypeStruct((t, RET_WIDTH), BF16),
        grid=(nbatch, per_b),
        in_specs=[row(2 * RET_QK), row(RET_WIDTH), row(RET_WIDTH), row(1), full(inv), full(gain_row)],
        out_specs=row(RET_WIDTH),
        scratch_shapes=[pltpu.VMEM((RET_HEADS, RET_DK, RET_DV), F32)],
        compiler_params=_cparams(("arbitrary", "arbitrary")),
        name="ret_mixer",
    )(rqk, rv, rg, pos, inv, gain_row)


def _pack_halves(h):
    lo = pltpu.bitcast(h[:, :HALF_D].astype(BF16).astype(F32), I32)
    hi = pltpu.bitcast(h[:, HALF_D:].astype(BF16).astype(F32), I32)
    return (hi & jnp.int32(-65536)) | lax.shift_right_logical(lo, 16)


def _unpack_halves(w):
    lo = pltpu.bitcast(lax.shift_left(w, 16), F32).astype(BF16)
    hi = pltpu.bitcast(w & jnp.int32(-65536), F32).astype(BF16)
    return lo, hi


def _mix_post_kernel(y_ref, x_ref, mod_ref, gpost_ref, gpre_ref, rwh_ref, rwl_ref, rb_ref,
                     xo_ref, hu_ref, ti_ref, tw_ref):
    m = mod_ref[...]
    x = x_ref[...] + m[2:3] * (_rms(y_ref[...]) * gpost_ref[...])
    xo_ref[...] = x
    h = _rms(x) * gpre_ref[...] * (1.0 + m[4:5]) + m[3:4]
    hu_ref[...] = _pack_halves(h)
    hh, hl = _split_bf16(h)
    logits = _dot(hh, rwh_ref[...]) + _dot(hl, rwh_ref[...]) + _dot(hh, rwl_ref[...]) + rb_ref[...]
    lane = lax.broadcasted_iota(I32, logits.shape, 1)
    vals, idxs = [], []
    for _ in range(TOP_K):
        mx = jnp.max(logits, axis=-1, keepdims=True)
        ix = jnp.min(jnp.where(logits == mx, lane, LANES), axis=-1, keepdims=True)
        vals.append(mx)
        idxs.append(ix)
        logits = jnp.where(lane == ix, -jnp.inf, logits)
    es = [jnp.exp(v - vals[0]) for v in vals]
    inv_sum = 1.0 / (es[0] + es[1] + es[2] + es[3])
    ti = jnp.zeros(lane.shape, I32)
    tw = jnp.zeros(lane.shape, F32)
    for kk in range(TOP_K):
        ti = jnp.where(lane == kk, idxs[kk], ti)
        tw = jnp.where(lane == kk, es[kk] * inv_sum, tw)
    ti_ref[...] = ti
    tw_ref[...] = tw


def _mix_post(y, x2, mod6, gpost, gpre, rw_hi, rw_lo, rb, seq):
    t, d = x2.shape
    per_b = seq // TM_ROW
    row = lambda w: pl.BlockSpec((TM_ROW, w), lambda i: (i, 0))
    full = lambda a: pl.BlockSpec(a.shape, lambda i: (0,) * a.ndim)
    gpost, gpre = gpost.reshape(1, d), gpre.reshape(1, d)
    return pl.pallas_call(
        _mix_post_kernel,
        out_shape=(jax.ShapeDtypeStruct((t, d), F32), jax.ShapeDtypeStruct((t, HALF_D), I32),
                   jax.ShapeDtypeStruct((t, LANES), I32), jax.ShapeDtypeStruct((t, LANES), F32)),
        grid=(t // TM_ROW,),
        in_specs=[row(d), row(d), pl.BlockSpec((None, N_MOD, d), lambda i: (i // per_b, 0, 0)),
                  full(gpost), full(gpre), full(rw_hi), full(rw_lo), full(rb)],
        out_specs=(row(d), row(HALF_D), row(LANES), row(LANES)),
        compiler_params=_cparams(("arbitrary",)),
        name="mix_post_router",
    )(y, x2, mod6, gpost, gpre, rw_hi, rw_lo, rb)


def _route(top_idx, n_tiles):
    t = top_idx.shape[0]
    e_flat = top_idx.reshape(-1)
    onehot = (e_flat[:, None] == jnp.arange(N_EXPERTS, dtype=I32)[None, :]).astype(I32)
    csum = jnp.cumsum(onehot, axis=0)
    rank = jnp.sum(onehot * (csum - 1), axis=1)
    counts = csum[-1]
    padded = ((counts + TM_MOE - 1) // TM_MOE) * TM_MOE
    ends = jnp.cumsum(padded)
    starts = ends - padded
    dest = starts[e_flat] + rank
    row_src = jnp.zeros((n_tiles * TM_MOE,), I32).at[dest].set(jnp.arange(t * TOP_K, dtype=I32) // TOP_K)
    n_used = ends[-1] // TM_MOE
    tile_ids = jnp.minimum(jnp.arange(n_tiles, dtype=I32), n_used - 1)
    tile_expert = jnp.searchsorted(ends, tile_ids * TM_MOE, side="right").astype(I32)
    pos_t = dest.reshape(t, TOP_K).T.reshape(-1)
    return row_src, tile_expert, n_used.reshape(1).astype(I32), pos_t


def _gather_kernel(src_ref, h_ref, o_ref, sem):
    base = pl.program_id(0) * GATHER_ROWS

    def body(r, carry):
        s = src_ref[base + r]
        pltpu.make_async_copy(h_ref.at[pl.ds(s, 1)], o_ref.at[pl.ds(base + r, 1)], sem).start()
        return carry

    lax.fori_loop(0, GATHER_ROWS, body, 0, unroll=8)
    pltpu.make_async_copy(h_ref.at[pl.ds(0, GATHER_ROWS)], o_ref.at[pl.ds(base, GATHER_ROWS)], sem).wait()


def _gather_rows(row_src, hu):
    m = row_src.shape[0]
    return pl.pallas_call(
        _gather_kernel,
        out_shape=jax.ShapeDtypeStruct((m, hu.shape[1]), hu.dtype),
        grid_spec=pltpu.PrefetchScalarGridSpec(
            num_scalar_prefetch=1,
            grid=(m // GATHER_ROWS,),
            in_specs=[pl.BlockSpec(memory_space=pl.ANY)],
            out_specs=pl.BlockSpec(memory_space=pl.ANY),
            scratch_shapes=[pltpu.SemaphoreType.DMA],
        ),
        compiler_params=_cparams(("arbitrary",)),
        name="moe_gather",
    )(row_src, hu)


def _new_weights(te_ref, m):
    prev = te_ref[jnp.maximum(m - 1, 0)]
    return jnp.logical_or(m == 0, te_ref[m] != prev)


def _gu_kernel(te_ref, nu_ref, x_ref, wg_ref, wu_ref, bg_ref, bu_ref, o_ref, wgb_ref, wub_ref):
    m = pl.program_id(1)

    @pl.when(_new_weights(te_ref, m))
    def _():
        wgb_ref[...] = wg_ref[...].astype(BF16)
        wub_ref[...] = wu_ref[...].astype(BF16)

    @pl.when(m < nu_ref[0])
    def _():
        lo, hi = _unpack_halves(x_ref[...])
        gate = _dot(lo, wgb_ref[0:HALF_D, :]) + _dot(hi, wgb_ref[HALF_D:, :]) + bg_ref[...]
        up = _dot(lo, wub_ref[0:HALF_D, :]) + _dot(hi, wub_ref[HALF_D:, :]) + bu_ref[...]
        gate = jnp.minimum(gate, SWIGLU_LIMIT)
        up = jnp.clip(up, -SWIGLU_LIMIT, SWIGLU_LIMIT)
        o_ref[...] = ((up + 1.0) * gate * jax.nn.sigmoid(SWIGLU_ALPHA * gate)).astype(o_ref.dtype)

    @pl.when(m >= nu_ref[0])
    def _():
        o_ref[...] = jnp.zeros_like(o_ref)


def _expert_gu(tile_expert, n_used, xs, w_gu, b_gu4, layer):
    mrows = xs.shape[0]
    nt = mrows // TM_MOE
    nn = D_EXPERT // TN_GU
    row_i = lambda n, m, te, nu: (jnp.minimum(m, nu[0] - 1), 0)
    return pl.pallas_call(
        _gu_kernel,
        out_shape=jax.ShapeDtypeStruct((mrows, D_EXPERT), BF16),
        grid_spec=pltpu.PrefetchScalarGridSpec(
            num_scalar_prefetch=2,
            grid=(nn, nt),
            in_specs=[
                pl.BlockSpec((TM_MOE, HALF_D), row_i),
                pl.BlockSpec((None, None, D_MODEL, TN_GU), lambda n, m, te, nu: (layer, te[m], 0, n)),
                pl.BlockSpec((None, None, D_MODEL, TN_GU), lambda n, m, te, nu: (layer, te[m], 0, nn + n)),
                pl.BlockSpec((None, None, 1, TN_GU), lambda n, m, te, nu: (layer, te[m], 0, n)),
                pl.BlockSpec((None, None, 1, TN_GU), lambda n, m, te, nu: (layer, te[m], 0, nn + n)),
            ],
            out_specs=pl.BlockSpec((TM_MOE, TN_GU), lambda n, m, te, nu: (m, n)),
            scratch_shapes=[pltpu.VMEM((D_MODEL, TN_GU), BF16), pltpu.VMEM((D_MODEL, TN_GU), BF16)],
        ),
        compiler_params=_cparams(("arbitrary", "arbitrary")),
        name="moe_gate_up",
    )(tile_expert, n_used, xs, w_gu, w_gu, b_gu4, b_gu4)


def _down_kernel(te_ref, nu_ref, a_ref, w_ref, b_ref, o_ref, wb_ref):
    m = pl.program_id(1)

    @pl.when(_new_weights(te_ref, m))
    def _():
        wb_ref[...] = w_ref[...].astype(BF16)

    @pl.when(m < nu_ref[0])
    def _():
        o_ref[...] = _dot(a_ref[...], wb_ref[...]) + b_ref[...]

    @pl.when(m >= nu_ref[0])
    def _():
        o_ref[...] = jnp.zeros_like(o_ref)


def _expert_down(tile_expert, n_used, act, w_down, b_down4, layer):
    mrows = act.shape[0]
    nt = mrows // TM_MOE
    return pl.pallas_call(
        _down_kernel,
        out_shape=jax.ShapeDtypeStruct((mrows, D_MODEL), F32),
        grid_spec=pltpu.PrefetchScalarGridSpec(
            num_scalar_prefetch=2,
            grid=(D_MODEL // TN_DOWN, nt),
            in_specs=[
                pl.BlockSpec((TM_MOE, D_EXPERT), lambda n, m, te, nu: (jnp.minimum(m, nu[0] - 1), 0)),
                pl.BlockSpec((None, None, D_EXPERT, TN_DOWN), lambda n, m, te, nu: (layer, te[m], 0, n)),
                pl.BlockSpec((None, None, 1, TN_DOWN), lambda n, m, te, nu: (layer, te[m], 0, n)),
            ],
            out_specs=pl.BlockSpec((TM_MOE, TN_DOWN), lambda n, m, te, nu: (m, n)),
            scratch_shapes=[pltpu.VMEM((D_EXPERT, TN_DOWN), BF16)],
        ),
        compiler_params=_cparams(("arbitrary", "arbitrary")),
        name="moe_down",
    )(tile_expert, n_used, act, w_down, b_down4)


def _combine_kernel(pos_ref, ys_ref, tw_ref, x_ref, mod_ref, gain_ref, o_ref, buf_ref, sem):
    i = pl.program_id(0)
    n = pl.num_programs(0)
    tm = x_ref.shape[0]
    t_total = n * tm
    rows = TOP_K * tm

    def issue(tile, slot):
        for kk in range(TOP_K):
            def body(r, carry, kk=kk):
                p = pos_ref[kk * t_total + tile * tm + r]
                pltpu.make_async_copy(ys_ref.at[pl.ds(p, 1)], buf_ref.at[slot, pl.ds(kk * tm + r, 1)],
                                      sem.at[slot]).start()
                return carry
            lax.fori_loop(0, tm, body, 0, unroll=8)

    @pl.when(i == 0)
    def _():
        issue(0, 0)

    @pl.when(i + 1 < n)
    def _():
        issue(i + 1, (i + 1) % 2)

    slot = i % 2
    pltpu.make_async_copy(ys_ref.at[pl.ds(0, rows)], buf_ref.at[slot], sem.at[slot]).wait()
    tw = tw_ref[...]
    y = tw[:, 0:1] * buf_ref[slot, 0:tm, :]
    for kk in range(1, TOP_K):
        y += tw[:, kk:kk + 1] * buf_ref[slot, kk * tm:(kk + 1) * tm, :]
    m = mod_ref[...]
    o_ref[...] = x_ref[...] + m[5:6] * (_rms(y) * gain_ref[...])


def _combine(pos_t, ys, top_w, x2, mod6, gain, seq):
    t, d = x2.shape
    per_b = seq // TM_CMB
    gain = gain.reshape(1, d)
    return pl.pallas_call(
        _combine_kernel,
        out_shape=jax.ShapeDtypeStruct((t, d), F32),
        grid_spec=pltpu.PrefetchScalarGridSpec(
            num_scalar_prefetch=1,
            grid=(t // TM_CMB,),
            in_specs=[
                pl.BlockSpec(memory_space=pl.ANY),
                pl.BlockSpec((TM_CMB, LANES), lambda i, pos: (i, 0)),
                pl.BlockSpec((TM_CMB, d), lambda i, pos: (i, 0)),
                pl.BlockSpec((None, N_MOD, d), lambda i, pos: (i // per_b, 0, 0)),
                pl.BlockSpec((1, d), lambda i, pos: (0, 0)),
            ],
            out_specs=pl.BlockSpec((TM_CMB, d), lambda i, pos: (i, 0)),
            scratch_shapes=[pltpu.VMEM((2, TOP_K * TM_CMB, d), F32), pltpu.SemaphoreType.DMA((2,))],
        ),
        compiler_params=_cparams(("arbitrary",)),
        name="moe_combine",
    )(pos_t, ys, top_w, x2, mod6, gain)


def kernel(x, c, positions, ada_w, ada_b, mix_pre_gain, mix_post_gain, ffn_pre_gain, ffn_post_gain,
           w_in, w_out, ssm_lam_re, ssm_lam_im, ssm_log_step, ssm_b_re, ssm_b_im, ssm_c_re, ssm_c_im,
           ssm_d, ssm_glu_w, ssm_glu_b, gla_wa, gla_ba, gla_norm_gain, ret_norm_gain,
           router_w, router_b, exp_w_gu, exp_b_gu, exp_w_down, exp_b_down):
    nbatch, seq, d = x.shape
    t = nbatch * seq
    depth = ada_w.shape[0]
    n_tiles = (t * TOP_K) // TM_MOE + N_EXPERTS

    x2 = x.reshape(t, d)
    pos = positions.reshape(t, 1)
    half = RET_DK // 2
    inv = jnp.power(ROPE_BASE, -jnp.arange(half, dtype=F32) / half)
    inv = jnp.concatenate([inv, inv]).reshape(1, RET_DK)
    mod = _ada(c, ada_w, ada_b)

    w_tail = jnp.concatenate(
        [w_in[:, :, OFF_TAIL:], w_in[:, :, OFF_GA:OFF_TAIL],
         jnp.zeros((depth, d, LANES - GLA_RANK), F32)], axis=-1).astype(BF16)
    rw = jnp.pad(router_w, ((0, 0), (0, 0), (0, LANES - N_EXPERTS)))
    rw_hi = rw.astype(BF16)
    rw_lo = (rw - rw_hi.astype(F32)).astype(BF16)
    rb = jnp.pad(router_b, ((0, 0), (0, LANES - N_EXPERTS)), constant_values=-1e30)
    b_gu4 = exp_b_gu.reshape(depth, N_EXPERTS, 1, 2 * D_EXPERT)
    b_down4 = exp_b_down.reshape(depth, N_EXPERTS, 1, d)
    glu_w = ssm_glu_w.astype(BF16)

    for l in range(depth):
        mod6 = mod[l].reshape(nbatch, N_MOD, d)
        h = _prenorm(x2, mix_pre_gain[l], mod6, seq, shift_i=0, scale_i=1)
        u = _mm(h, w_in, l, OFF_U, SSM_WIDTH)
        gqk = _mm(h, w_in, l, OFF_GQK, 2 * GLA_QK)
        gv = _mm(h, w_in, l, OFF_GV, GLA_WIDTH)
        gg = _mm(h, w_in, l, OFF_GG, GLA_WIDTH)
        rqk = _mm(h, w_tail, l, 0, 2 * RET_QK)
        rv = _mm(h, w_tail, l, 2 * RET_QK, RET_WIDTH)
        rg = _mm(h, w_tail, l, 2 * RET_QK + RET_WIDTH, RET_WIDTH)
        ga = _mm(h, w_tail, l, 2 * RET_QK + 2 * RET_WIDTH, LANES, tn=LANES)

        b_mat, c_mat, lam_tab = _ssm_params(ssm_lam_re[l], ssm_lam_im[l], ssm_log_step[l], ssm_b_re[l],
                                            ssm_b_im[l], ssm_c_re[l], ssm_c_im[l])
        y_ssm = _ssm(u, b_mat, c_mat, lam_tab, ssm_d[l], glu_w[l], ssm_glu_b[l], nbatch)
        y_gla = _gla(gqk, gv, gg, ga, gla_wa[l], gla_ba[l], gla_norm_gain[l], nbatch)
        y_ret = _ret(rqk, rv, rg, pos, inv, ret_norm_gain[l], nbatch)
        y = _mm_out(y_ssm, y_gla, y_ret, w_out, l)

        x2, hu, top_idx, top_w = _mix_post(y, x2, mod6, mix_post_gain[l], ffn_pre_gain[l],
                                           rw_hi[l], rw_lo[l], rb[l:l + 1], seq)
        row_src, tile_expert, n_used, pos_t = _route(top_idx[:, :TOP_K], n_tiles)
        xs = _gather_rows(row_src, hu)
        act = _expert_gu(tile_expert, n_used, xs, exp_w_gu, b_gu4, l)
        ys = _expert_down(tile_expert, n_used, act, exp_w_down, b_down4, l)
        x2 = _combine(pos_t, ys, top_w, x2, mod6, ffn_post_gain[l], seq)
    return x2.reshape(nbatch, seq, d)
```

```python
import functools
import math

import jax
import jax.numpy as jnp
from jax import lax
from jax.experimental import pallas as pl
from jax.experimental.pallas import tpu as pltpu

F32 = jnp.float32
BF16 = jnp.bfloat16
I32 = jnp.int32

D_MODEL = 4096
N_MOD = 6
SSM_WIDTH = 1024
SSM_GROUP = 16
SSM_GROUPS = 64
SSM_STATE = 64
SSM_LANES = SSM_GROUPS * SSM_STATE
GLA_WIDTH = 1536
GLA_HEADS = 4
GLA_DV = 384
GLA_DK = 192
GLA_QK = 768
GLA_RANK = 16
GLA_TAU = 16.0
GLA_CHUNK = 64
RET_WIDTH = 1536
RET_HEADS = 6
RET_DV = 256
RET_DK = 128
RET_QK = 768
ROPE_BASE = 10000.0
N_EXPERTS = 32
TOP_K = 4
D_EXPERT = 768
SWIGLU_LIMIT = 7.0
SWIGLU_ALPHA = 1.702
EPS = 1e-6
OFF_U = 0
OFF_GQK = 1024
OFF_GV = 2560
OFF_GG = 4096
OFF_GA = 5632
OFF_TAIL = 5648

LANES = 128
SUBLANES = 8
VMEM_LIMIT = 52 * 1024 * 1024

TM_ROW = 256
TM_MM = 512
TN_MM = 512
TB_SSM = 256
SSM_LC = 512
SSM_GB = 16
TB_GLA = 256
TB_RET = 256
TM_MOE = 256
TN_GU = 384
TN_DOWN = 2048
TM_CMB = 128
GATHER_ROWS = 512


def _cparams(sem):
    return pltpu.CompilerParams(dimension_semantics=sem, vmem_limit_bytes=VMEM_LIMIT)


def _dot(a, b):
    return jnp.dot(a, b, preferred_element_type=F32)


def _dot_nt(a, b):
    return lax.dot_general(a, b, (((1,), (1,)), ((), ())), preferred_element_type=F32)


def _dot_tn(a, b):
    return lax.dot_general(a, b, (((0,), (0,)), ((), ())), preferred_element_type=F32)


def _split_bf16(x):
    hi = x.astype(BF16)
    lo = (x - hi.astype(F32)).astype(BF16)
    return hi, lo


def _dot3(a, b):
    ah, al = _split_bf16(a)
    bh, bl = _split_bf16(b)
    return _dot(ah, bh) + _dot(al, bh) + _dot(ah, bl)


def _ada_kernel(c_ref, w_ref, b_ref, o_ref, cab_ref):
    nb = c_ref.shape[0]
    d, tn = w_ref.shape
    rk = 64

    @pl.when(jnp.logical_and(pl.program_id(0) == 0, pl.program_id(1) == 0))
    def _():
        for b in range(nb):
            cc = c_ref[b]
            cab_ref[b] = jnp.broadcast_to(cc * jax.nn.sigmoid(cc), (d, LANES))

    def body(i, accs):
        r0 = pl.multiple_of(i * rk, rk)
        wk = w_ref[pl.ds(r0, rk), :]
        out = []
        for b in range(nb):
            ca = cab_ref[b, pl.ds(r0, rk), :]
            prod = jnp.concatenate([wk[:, q * LANES:(q + 1) * LANES] * ca for q in range(tn // LANES)], axis=1)
            out.append(accs[b] + jnp.sum(prod.reshape(rk // SUBLANES, SUBLANES, tn), axis=0))
        return tuple(out)

    accs = lax.fori_loop(0, d // rk, body, tuple(jnp.zeros((SUBLANES, tn), F32) for _ in range(nb)))
    for b in range(nb):
        o_ref[b:b + 1, :] = jnp.sum(accs[b], axis=0, keepdims=True) + b_ref[...]


def _ada(c, ada_w, ada_b):
    n_layers, d, n = ada_w.shape
    nb = c.shape[0]
    tn = 512
    return pl.pallas_call(
        _ada_kernel,
        out_shape=jax.ShapeDtypeStruct((n_layers, nb, n), F32),
        grid=(n_layers, n // tn),
        in_specs=[
            pl.BlockSpec((nb, d, 1), lambda l, j: (0, 0, 0)),
            pl.BlockSpec((None, d, tn), lambda l, j: (l, 0, j)),
            pl.BlockSpec((None, 1, tn), lambda l, j: (l, 0, j)),
        ],
        out_specs=pl.BlockSpec((None, nb, tn), lambda l, j: (l, 0, j)),
        scratch_shapes=[pltpu.VMEM((nb, d, LANES), F32)],
        compiler_params=_cparams(("arbitrary", "arbitrary")),
        name="ada_mod",
    )(c.reshape(nb, d, 1), ada_w, ada_b.reshape(n_layers, 1, n))


def _rms(x):
    return x * lax.rsqrt(jnp.mean(x * x, axis=-1, keepdims=True) + EPS)


def _prenorm_kernel(x_ref, g_ref, mod_ref, o_ref, *, shift_i, scale_i):
    m = mod_ref[...]
    h = _rms(x_ref[...]) * g_ref[...]
    o_ref[...] = (h * (1.0 + m[scale_i:scale_i + 1]) + m[shift_i:shift_i + 1]).astype(o_ref.dtype)


def _prenorm(x2, gain, mod6, seq, *, shift_i, scale_i):
    t, d = x2.shape
    per_b = seq // TM_ROW
    return pl.pallas_call(
        functools.partial(_prenorm_kernel, shift_i=shift_i, scale_i=scale_i),
        out_shape=jax.ShapeDtypeStruct((t, d), BF16),
        grid=(t // TM_ROW,),
        in_specs=[
            pl.BlockSpec((TM_ROW, d), lambda i: (i, 0)),
            pl.BlockSpec((1, d), lambda i: (0, 0)),
            pl.BlockSpec((None, N_MOD, d), lambda i: (i // per_b, 0, 0)),
        ],
        out_specs=pl.BlockSpec((TM_ROW, d), lambda i: (i, 0)),
        compiler_params=_cparams(("arbitrary",)),
        name="prenorm",
    )(x2, gain.reshape(1, d), mod6)


def _mm_kernel(x_ref, w_ref, o_ref, wbf_ref):
    @pl.when(pl.program_id(1) == 0)
    def _():
        wbf_ref[...] = w_ref[...].astype(BF16)

    o_ref[...] = _dot_nt(x_ref[...], wbf_ref[...]).astype(o_ref.dtype)


def _mm(x, w_t, layer, row_off, width, *, tn=TN_MM, out_dtype=F32):
    t, k = x.shape
    jb = row_off // tn
    return pl.pallas_call(
        _mm_kernel,
        out_shape=jax.ShapeDtypeStruct((t, width), out_dtype),
        grid=(width // tn, t // TM_MM),
        in_specs=[
            pl.BlockSpec((TM_MM, k), lambda j, i: (i, 0)),
            pl.BlockSpec((None, tn, k), lambda j, i: (layer, jb + j, 0)),
        ],
        out_specs=pl.BlockSpec((TM_MM, tn), lambda j, i: (i, j)),
        scratch_shapes=[pltpu.VMEM((tn, k), BF16)],
        compiler_params=_cparams(("arbitrary", "arbitrary")),
        name="proj_in",
    )(x, w_t)


def _mm_shift_kernel(x_ref, wa_ref, wb_ref, o_ref, wbf_ref, *, shift):
    @pl.when(pl.program_id(1) == 0)
    def _():
        tn = wa_ref.shape[0]
        wbf_ref[0:tn - shift, :] = wa_ref[shift:, :].astype(BF16)
        wbf_ref[tn - shift:, :] = wb_ref[0:shift, :].astype(BF16)

    o_ref[...] = _dot_nt(x_ref[...], wbf_ref[...]).astype(o_ref.dtype)


def _mm_shift(x, w_t, layer, row_off, width, *, out_dtype=F32):
    t, k = x.shape
    tn = TN_MM
    jb, shift = divmod(row_off, tn)
    assert shift % 16 == 0
    return pl.pallas_call(
        functools.partial(_mm_shift_kernel, shift=shift),
        out_shape=jax.ShapeDtypeStruct((t, width), out_dtype),
        grid=(width // tn, t // TM_MM),
        in_specs=[
            pl.BlockSpec((TM_MM, k), lambda j, i: (i, 0)),
            pl.BlockSpec((None, tn, k), lambda j, i: (layer, jb + j, 0)),
            pl.BlockSpec((None, tn, k), lambda j, i: (layer, jb + j + 1, 0)),
        ],
        out_specs=pl.BlockSpec((TM_MM, tn), lambda j, i: (i, j)),
        scratch_shapes=[pltpu.VMEM((tn, k), BF16)],
        compiler_params=_cparams(("arbitrary", "arbitrary")),
        name="proj_in_shift",
    )(x, w_t, w_t)


def _mm3_kernel(a_ref, b_ref, c_ref, w_ref, o_ref, wbf_ref):
    @pl.when(pl.program_id(1) == 0)
    def _():
        wbf_ref[...] = w_ref[...].astype(BF16)

    ka, kb = a_ref.shape[1], b_ref.shape[1]
    acc = _dot(a_ref[...], wbf_ref[0:ka, :])
    acc += _dot(b_ref[...], wbf_ref[ka:ka + kb, :])
    acc += _dot(c_ref[...], wbf_ref[ka + kb:, :])
    o_ref[...] = acc


def _mm_out(ya, yb, yc, w_full, layer):
    t = ya.shape[0]
    k, n = w_full.shape[1], w_full.shape[2]
    return pl.pallas_call(
        _mm3_kernel,
        out_shape=jax.ShapeDtypeStruct((t, n), F32),
        grid=(n // TN_MM, t // TM_MM),
        in_specs=[
            pl.BlockSpec((TM_MM, ya.shape[1]), lambda j, i: (i, 0)),
            pl.BlockSpec((TM_MM, yb.shape[1]), lambda j, i: (i, 0)),
            pl.BlockSpec((TM_MM, yc.shape[1]), lambda j, i: (i, 0)),
            pl.BlockSpec((None, k, TN_MM), lambda j, i: (layer, 0, j)),
        ],
        out_specs=pl.BlockSpec((TM_MM, TN_MM), lambda j, i: (i, j)),
        scratch_shapes=[pltpu.VMEM((k, TN_MM), BF16)],
        compiler_params=_cparams(("arbitrary", "arbitrary")),
        name="proj_out",
    )(ya, yb, yc, w_full)


def _ssm_params(lam_re, lam_im, log_step, b_re, b_im, c_re, c_im):
    step = jnp.exp(log_step)[:, None]
    ar, ai = lam_re * step, lam_im * step
    mag = jnp.exp(ar)
    lbr, lbi = mag * jnp.cos(ai), mag * jnp.sin(ai)
    den = lam_re * lam_re + lam_im * lam_im
    cfr = ((lbr - 1.0) * lam_re + lbi * lam_im) / den
    cfi = (lbi * lam_re - (lbr - 1.0) * lam_im) / den
    bbr = cfr[..., None] * b_re - cfi[..., None] * b_im
    bbi = cfr[..., None] * b_im + cfi[..., None] * b_re
    nj = SSM_GROUPS // SSM_GB
    eye = jnp.eye(SSM_GB, dtype=F32)

    def bmat(bb):
        bb = bb.reshape(nj, SSM_GB, SSM_STATE, SSM_GROUP)
        m = jnp.einsum("ab,jbph->jahbp", eye, bb)
        return m.reshape(nj, SSM_GB * SSM_GROUP, SSM_GB * SSM_STATE)

    def cmat(cc):
        cc = cc.reshape(nj, SSM_GB, SSM_GROUP, SSM_STATE)
        m = jnp.einsum("ab,jbhp->japbh", eye, cc)
        return m.reshape(nj, SSM_GB * SSM_STATE, SSM_GB * SSM_GROUP)

    b_mat = jnp.concatenate([bmat(bbr), bmat(bbi)], axis=-1).astype(BF16)
    c_mat = jnp.concatenate([cmat(c_re), -cmat(c_im)], axis=1).astype(BF16)

    def power(kk):
        m = jnp.exp(kk * ar[None])
        return ((m * jnp.cos(kk * ai[None])).reshape(-1, SSM_LANES),
                (m * jnp.sin(kk * ai[None])).reshape(-1, SSM_LANES))

    rows = jnp.arange(SUBLANES, dtype=F32)[:, None]
    shifts = []
    for s in (1, 2, 4):
        pr, pi = power(jnp.full((1, 1, 1), float(s), F32))
        keep = rows >= s
        shifts += [jnp.where(keep, pr, 0.0), jnp.where(keep, pi, 0.0)]
    pr, pi = power(jnp.arange(1, SUBLANES + 1, dtype=F32)[:, None, None])
    lam_tab = jnp.stack(shifts + [pr, pi])
    return b_mat, c_mat, lam_tab


def _ssm_kernel(u_ref, bm_ref, cm_ref, lt_ref, d_ref, gw_ref, gb_ref, o_ref, xr_ref, xi_ref, st_ref):
    tb = u_ref.shape[0]
    nj = bm_ref.shape[0]
    gw = SSM_GB * SSM_GROUP
    sw = SSM_GB * SSM_STATE

    @pl.when(pl.program_id(1) == 0)
    def _():
        st_ref[...] = jnp.zeros_like(st_ref)

    u = u_ref[...]
    ub = u.astype(BF16)
    for j in range(nj):
        bu = _dot(ub[:, j * gw:(j + 1) * gw], bm_ref[j])
        xr_ref[:, j * sw:(j + 1) * sw] = bu[:, :sw]
        xi_ref[:, j * sw:(j + 1) * sw] = bu[:, sw:]

    def row_body(r, carry):
        rs = pl.ds(pl.multiple_of(r * SUBLANES, SUBLANES), SUBLANES)
        for c in range(SSM_LANES // SSM_LC):
            sl = slice(c * SSM_LC, (c + 1) * SSM_LC)
            xr = xr_ref[rs, sl]
            xi = xi_ref[rs, sl]
            for k, s in enumerate((1, 2, 4)):
                lr, li = lt_ref[2 * k, :, sl], lt_ref[2 * k + 1, :, sl]
                sr = pltpu.roll(xr, s, axis=0)
                si = pltpu.roll(xi, s, axis=0)
                xr, xi = xr + (lr * sr - li * si), xi + (lr * si + li * sr)
            pr, pi = lt_ref[6, :, sl], lt_ref[7, :, sl]
            cr, ci = st_ref[0:1, sl], st_ref[1:2, sl]
            xr, xi = xr + (pr * cr - pi * ci), xi + (pr * ci + pi * cr)
            xr_ref[rs, sl] = xr
            xi_ref[rs, sl] = xi
            st_ref[0:1, sl] = xr[SUBLANES - 1:, :]
            st_ref[1:2, sl] = xi[SUBLANES - 1:, :]
        return carry

    lax.fori_loop(0, tb // SUBLANES, row_body, 0)

    ys = []
    for j in range(nj):
        xs = jnp.concatenate([xr_ref[:, j * sw:(j + 1) * sw], xi_ref[:, j * sw:(j + 1) * sw]], axis=-1)
        ys.append(_dot(xs.astype(BF16), cm_ref[j]))
    y = jnp.concatenate(ys, axis=-1) + d_ref[...] * u
    y = jax.nn.gelu(y)
    gate = _dot(y.astype(BF16), gw_ref[...]) + gb_ref[...]
    o_ref[...] = (y * jax.nn.sigmoid(gate)).astype(o_ref.dtype)


def _ssm(u, b_mat, c_mat, lam_tab, d_skip, glu_w, glu_b, nbatch):
    t, w = u.shape
    per_b = t // nbatch // TB_SSM
    full = lambda a: pl.BlockSpec(a.shape, lambda b, i: (0,) * a.ndim)
    d_row = d_skip.reshape(1, w)
    gb_row = glu_b.reshape(1, w)
    return pl.pallas_call(
        _ssm_kernel,
        out_shape=jax.ShapeDtypeStruct((t, w), BF16),
        grid=(nbatch, per_b),
        in_specs=[
            pl.BlockSpec((TB_SSM, w), lambda b, i: (b * per_b + i, 0)),
            full(b_mat), full(c_mat), full(lam_tab), full(d_row), full(glu_w), full(gb_row),
        ],
        out_specs=pl.BlockSpec((TB_SSM, w), lambda b, i: (b * per_b + i, 0)),
        scratch_shapes=[
            pltpu.VMEM((TB_SSM, SSM_LANES), F32),
            pltpu.VMEM((TB_SSM, SSM_LANES), F32),
            pltpu.VMEM((2, SSM_LANES), F32),
        ],
        compiler_params=_cparams(("arbitrary", "arbitrary")),
        name="s5_mixer",
    )(u, b_mat, c_mat, lam_tab, d_row, glu_w, gb_row)


def _log_sigmoid(z):
    return jnp.minimum(z, 0.0) - jnp.log1p(jnp.exp(-jnp.abs(z)))


def _silu(x):
    return x * jax.nn.sigmoid(x)


def _gla_kernel(qk_ref, v_ref, g_ref, a_ref, wa_ref, ba_ref, gain_ref, o_ref, st_ref):
    tb = qk_ref.shape[0]
    ck = GLA_CHUNK

    @pl.when(pl.program_id(1) == 0)
    def _():
        st_ref[...] = jnp.zeros_like(st_ref)

    log_a = _log_sigmoid(_dot3(a_ref[:, :GLA_RANK], wa_ref[...]) + ba_ref[...]) * (1.0 / GLA_TAU)
    ri = lax.broadcasted_iota(I32, (ck, ck), 0)
    ci = lax.broadcasted_iota(I32, (ck, ck), 1)
    causal = ri >= ci
    tri = jnp.where(causal, 1.0, 0.0).astype(BF16)
    for c in range(tb // ck):
        rows = slice(c * ck, (c + 1) * ck)
        la_hi, la_lo = _split_bf16(log_a[rows])
        dec = _dot(tri, la_hi) + _dot(tri, la_lo)
        for h in range(GLA_HEADS):
            kcol = slice(h * GLA_DK, (h + 1) * GLA_DK)
            vcol = slice(h * GLA_DV, (h + 1) * GLA_DV)
            b = dec[:, kcol]
            b_last = b[ck - 1:ck, :]
            q = qk_ref[rows, kcol] * (GLA_DK ** -0.5)
            k = qk_ref[rows, GLA_QK + h * GLA_DK:GLA_QK + (h + 1) * GLA_DK]
            v = v_ref[rows, vcol].astype(BF16)
            q_dec = (q * jnp.exp(b)).astype(BF16)
            k_intra = (k * jnp.exp(-b)).astype(BF16)
            k_state = (k * jnp.exp(b_last - b)).astype(BF16)
            scores = jnp.where(causal, _dot_nt(q_dec, k_intra), 0.0)
            st = st_ref[h]
            o = _dot(scores.astype(BF16), v) + _dot_nt(q_dec, st.astype(BF16))
            st_ref[h] = st * jnp.exp(b_last) + _dot_tn(v, k_state)
            o = _rms(o) * gain_ref[:, vcol] * _silu(g_ref[rows, vcol])
            o_ref[rows, vcol] = o.astype(o_ref.dtype)


def _gla(gqk, gv, gg, ga, wa, ba, gain, nbatch):
    t = gqk.shape[0]
    per_b = t // nbatch // TB_GLA
    row = lambda w: pl.BlockSpec((TB_GLA, w), lambda b, i: (b * per_b + i, 0))
    full = lambda a: pl.BlockSpec(a.shape, lambda b, i: (0,) * a.ndim)
    ba_row = ba.reshape(1, GLA_QK)
    gain_row = gain.reshape(1, GLA_WIDTH)
    return pl.pallas_call(
        _gla_kernel,
        out_shape=jax.ShapeDtypeStruct((t, GLA_WIDTH), BF16),
        grid=(nbatch, per_b),
        in_specs=[row(2 * GLA_QK), row(GLA_WIDTH), row(GLA_WIDTH), row(LANES),
                  full(wa), full(ba_row), full(gain_row)],
        out_specs=row(GLA_WIDTH),
        scratch_shapes=[pltpu.VMEM((GLA_HEADS, GLA_DV, GLA_DK), F32)],
        compiler_params=_cparams(("arbitrary", "arbitrary")),
        name="gla_mixer",
    )(gqk, gv, gg, ga, wa, ba_row, gain_row)


def _ret_kernel(qk_ref, v_ref, g_ref, pos_ref, inv_ref, gain_ref, o_ref, st_ref):
    tb = qk_ref.shape[0]

    @pl.when(pl.program_id(1) == 0)
    def _():
        st_ref[...] = jnp.zeros_like(st_ref)

    ang = pos_ref[...].astype(F32) * inv_ref[...]
    lane = lax.broadcasted_iota(I32, (1, RET_DK), 1)
    cos = jnp.cos(ang)
    sin = jnp.sin(ang) * jnp.where(lane < RET_DK // 2, -1.0, 1.0)
    ri = lax.broadcasted_iota(I32, (tb, tb), 0)
    ci = lax.broadcasted_iota(I32, (tb, tb), 1)
    rel = (ri - ci).astype(F32)
    idx = lax.broadcasted_iota(I32, (tb, 1), 0).astype(F32)
    for h in range(RET_HEADS):
        log_gamma = math.log1p(-(2.0 ** (-5.0 - h)))
        kcol = slice(h * RET_DK, (h + 1) * RET_DK)
        vcol = slice(h * RET_DV, (h + 1) * RET_DV)
        q = qk_ref[:, kcol]
        k = qk_ref[:, RET_QK + h * RET_DK:RET_QK + (h + 1) * RET_DK]
        q = (q * cos + pltpu.roll(q, RET_DK // 2, axis=1) * sin) * (RET_DK ** -0.5)
        k = k * cos + pltpu.roll(k, RET_DK // 2, axis=1) * sin
        v = v_ref[:, vcol].astype(BF16)
        decay = jnp.where(rel >= 0, jnp.exp(jnp.maximum(rel, 0.0) * log_gamma), 0.0)
        scores = _dot_nt(q.astype(BF16), k.astype(BF16)) * decay
        q_w = jnp.exp((idx + 1.0) * log_gamma)
        k_w = jnp.exp((tb - 1.0 - idx) * log_gamma)
        st = st_ref[h]
        o = _dot(scores.astype(BF16), v) + _dot((q * q_w).astype(BF16), st.astype(BF16))
        st_ref[h] = st * math.exp(tb * log_gamma) + _dot_tn((k * k_w).astype(BF16), v)
        o = _rms(o) * gain_ref[:, vcol] * _silu(g_ref[:, vcol])
        o_ref[:, vcol] = o.astype(o_ref.dtype)


def _ret(rqk, rv, rg, pos, inv, gain, nbatch):
    t = rqk.shape[0]
    per_b = t // nbatch // TB_RET
    row = lambda w: pl.BlockSpec((TB_RET, w), lambda b, i: (b * per_b + i, 0))
    full = lambda a: pl.BlockSpec(a.shape, lambda b, i: (0,) * a.ndim)
    gain_row = gain.reshape(1, RET_WIDTH)
    return pl.pallas_call(
        _ret_kernel,
        out_shape=jax.ShapeDtypeStruct((t, RET_WIDTH), BF16),
        grid=(nbatch, per_b),
        in_specs=[row(2 * RET_QK), row(RET_WIDTH), row(RET_WIDTH), row(1), full(inv), full(gain_row)],
        out_specs=row(RET_WIDTH),
        scratch_shapes=[pltpu.VMEM((RET_HEADS, RET_DK, RET_DV), F32)],
        compiler_params=_cparams(("arbitrary", "arbitrary")),
        name="ret_mixer",
    )(rqk, rv, rg, pos, inv, gain_row)


def _mix_post_kernel(y_ref, x_ref, mod_ref, gpost_ref, gpre_ref, rwh_ref, rwl_ref, rb_ref,
                     xo_ref, ho_ref, ti_ref, tw_ref):
    m = mod_ref[...]
    x = x_ref[...] + m[2:3] * (_rms(y_ref[...]) * gpost_ref[...])
    xo_ref[...] = x
    h = _rms(x) * gpre_ref[...] * (1.0 + m[4:5]) + m[3:4]
    ho_ref[...] = h
    hh, hl = _split_bf16(h)
    logits = _dot(hh, rwh_ref[...]) + _dot(hl, rwh_ref[...]) + _dot(hh, rwl_ref[...]) + rb_ref[...]
    lane = lax.broadcasted_iota(I32, logits.shape, 1)
    vals, idxs = [], []
    for _ in range(TOP_K):
        mx = jnp.max(logits, axis=-1, keepdims=True)
        ix = jnp.min(jnp.where(logits == mx, lane, LANES), axis=-1, keepdims=True)
        vals.append(mx)
        idxs.append(ix)
        logits = jnp.where(lane == ix, -jnp.inf, logits)
    es = [jnp.exp(v - vals[0]) for v in vals]
    inv_sum = 1.0 / (es[0] + es[1] + es[2] + es[3])
    ti = jnp.zeros(lane.shape, I32)
    tw = jnp.zeros(lane.shape, F32)
    for kk in range(TOP_K):
        ti = jnp.where(lane == kk, idxs[kk], ti)
        tw = jnp.where(lane == kk, es[kk] * inv_sum, tw)
    ti_ref[...] = ti
    tw_ref[...] = tw


def _mix_post(y, x2, mod6, gpost, gpre, rw_hi, rw_lo, rb, seq):
    t, d = x2.shape
    per_b = seq // TM_ROW
    row = lambda w: pl.BlockSpec((TM_ROW, w), lambda i: (i, 0))
    full = lambda a: pl.BlockSpec(a.shape, lambda i: (0,) * a.ndim)
    gpost, gpre = gpost.reshape(1, d), gpre.reshape(1, d)
    return pl.pallas_call(
        _mix_post_kernel,
        out_shape=(jax.ShapeDtypeStruct((t, d), F32), jax.ShapeDtypeStruct((t, d), F32),
                   jax.ShapeDtypeStruct((t, LANES), I32), jax.ShapeDtypeStruct((t, LANES), F32)),
        grid=(t // TM_ROW,),
        in_specs=[row(d), row(d), pl.BlockSpec((None, N_MOD, d), lambda i: (i // per_b, 0, 0)),
                  full(gpost), full(gpre), full(rw_hi), full(rw_lo), full(rb)],
        out_specs=(row(d), row(d), row(LANES), row(LANES)),
        compiler_params=_cparams(("arbitrary",)),
        name="mix_post_router",
    )(y, x2, mod6, gpost, gpre, rw_hi, rw_lo, rb)


def _route(top_idx, n_tiles):
    t = top_idx.shape[0]
    e_flat = top_idx.reshape(-1)
    onehot = e_flat[:, None] == jnp.arange(N_EXPERTS, dtype=I32)[None, :]
    blk = 256
    nblk = (t * TOP_K) // blk
    oh3 = onehot.astype(BF16).reshape(nblk, blk, N_EXPERTS)
    within = jnp.einsum("ij,bjk->bik", jnp.tril(jnp.ones((blk, blk), BF16)), oh3, preferred_element_type=F32)
    tot = within[:, -1, :]
    before = jnp.dot(jnp.tril(jnp.ones((nblk, nblk), BF16), -1), tot.astype(BF16), preferred_element_type=F32)
    csum = (within + before[:, None, :]).reshape(t * TOP_K, N_EXPERTS)
    counts = (before[-1] + tot[-1]).astype(I32)
    padded = ((counts + TM_MOE - 1) // TM_MOE) * TM_MOE
    ends = jnp.cumsum(padded)
    starts = ends - padded
    dest = jnp.sum(jnp.where(onehot, csum - 1.0 + starts.astype(F32)[None, :], 0.0), axis=1).astype(I32)
    row_src = jnp.zeros((n_tiles * TM_MOE,), I32).at[dest].set(jnp.arange(t * TOP_K, dtype=I32) // TOP_K)
    n_used = ends[-1] // TM_MOE
    tile_ids = jnp.minimum(jnp.arange(n_tiles, dtype=I32), n_used - 1)
    tile_expert = jnp.sum(ends[None, :] <= (tile_ids * TM_MOE)[:, None], axis=1).astype(I32)
    pos_t = dest.reshape(t, TOP_K).T.reshape(-1)
    return row_src, tile_expert, n_used.reshape(1).astype(I32), pos_t


def _gather_kernel(src_ref, h_ref, o_ref, buf_ref, sem):
    i = pl.program_id(0)
    n = pl.num_programs(0)

    def issue(step, slot):
        base = step * GATHER_ROWS

        def body(r, carry):
            s = src_ref[base + r]
            pltpu.make_async_copy(h_ref.at[pl.ds(s, 1)], buf_ref.at[slot, pl.ds(r, 1)], sem.at[slot]).start()
            return carry

        lax.fori_loop(0, GATHER_ROWS, body, 0, unroll=8)

    @pl.when(i == 0)
    def _():
        issue(0, 0)

    @pl.when(i + 1 < n)
    def _():
        issue(i + 1, (i + 1) % 2)

    slot = i % 2
    pltpu.make_async_copy(h_ref.at[pl.ds(0, GATHER_ROWS)], buf_ref.at[slot], sem.at[slot]).wait()
    o_ref[...] = buf_ref[slot].astype(o_ref.dtype)


def _gather_rows(row_src, h):
    m = row_src.shape[0]
    d = h.shape[1]
    return pl.pallas_call(
        _gather_kernel,
        out_shape=jax.ShapeDtypeStruct((m, d), BF16),
        grid_spec=pltpu.PrefetchScalarGridSpec(
            num_scalar_prefetch=1,
            grid=(m // GATHER_ROWS,),
            in_specs=[pl.BlockSpec(memory_space=pl.ANY)],
            out_specs=pl.BlockSpec((GATHER_ROWS, d), lambda i, src: (i, 0)),
            scratch_shapes=[pltpu.VMEM((2, GATHER_ROWS, d), F32), pltpu.SemaphoreType.DMA((2,))],
        ),
        compiler_params=_cparams(("arbitrary",)),
        name="moe_gather",
    )(row_src, h)


def _new_weights(te_ref, m):
    prev = te_ref[jnp.maximum(m - 1, 0)]
    return jnp.logical_or(m == 0, te_ref[m] != prev)


def _gu_kernel(te_ref, nu_ref, x_ref, wg_ref, wu_ref, bg_ref, bu_ref, o_ref, wgb_ref, wub_ref):
    m = pl.program_id(1)

    @pl.when(_new_weights(te_ref, m))
    def _():
        wgb_ref[...] = wg_ref[...].astype(BF16)
        wub_ref[...] = wu_ref[...].astype(BF16)

    @pl.when(m < nu_ref[0])
    def _():
        x = x_ref[...]
        gate = _dot(x, wgb_ref[...]) + bg_ref[...]
        up = _dot(x, wub_ref[...]) + bu_ref[...]
        gate = jnp.minimum(gate, SWIGLU_LIMIT)
        up = jnp.clip(up, -SWIGLU_LIMIT, SWIGLU_LIMIT)
        o_ref[...] = ((up + 1.0) * gate * jax.nn.sigmoid(SWIGLU_ALPHA * gate)).astype(o_ref.dtype)

    @pl.when(m >= nu_ref[0])
    def _():
        o_ref[...] = jnp.zeros_like(o_ref)


def _expert_gu(tile_expert, n_used, xs, w_gu, b_gu4, layer):
    mrows = xs.shape[0]
    nt = mrows // TM_MOE
    nn = D_EXPERT // TN_GU
    row_i = lambda n, m, te, nu: (jnp.minimum(m, nu[0] - 1), 0)
    return pl.pallas_call(
        _gu_kernel,
        out_shape=jax.ShapeDtypeStruct((mrows, D_EXPERT), BF16),
        grid_spec=pltpu.PrefetchScalarGridSpec(
            num_scalar_prefetch=2,
            grid=(nn, nt),
            in_specs=[
                pl.BlockSpec((TM_MOE, D_MODEL), row_i),
                pl.BlockSpec((None, None, D_MODEL, TN_GU), lambda n, m, te, nu: (layer, te[m], 0, n)),
                pl.BlockSpec((None, None, D_MODEL, TN_GU), lambda n, m, te, nu: (layer, te[m], 0, nn + n)),
                pl.BlockSpec((None, None, 1, TN_GU), lambda n, m, te, nu: (layer, te[m], 0, n)),
                pl.BlockSpec((None, None, 1, TN_GU), lambda n, m, te, nu: (layer, te[m], 0, nn + n)),
            ],
            out_specs=pl.BlockSpec((TM_MOE, TN_GU), lambda n, m, te, nu: (m, n)),
            scratch_shapes=[pltpu.VMEM((D_MODEL, TN_GU), BF16), pltpu.VMEM((D_MODEL, TN_GU), BF16)],
        ),
        compiler_params=_cparams(("arbitrary", "arbitrary")),
        name="moe_gate_up",
    )(tile_expert, n_used, xs, w_gu, w_gu, b_gu4, b_gu4)


def _down_kernel(te_ref, nu_ref, a_ref, w_ref, b_ref, o_ref, wb_ref):
    m = pl.program_id(1)

    @pl.when(_new_weights(te_ref, m))
    def _():
        wb_ref[...] = w_ref[...].astype(BF16)

    @pl.when(m < nu_ref[0])
    def _():
        o_ref[...] = _dot(a_ref[...], wb_ref[...]) + b_ref[...]

    @pl.when(m >= nu_ref[0])
    def _():
        o_ref[...] = jnp.zeros_like(o_ref)


def _expert_down(tile_expert, n_used, act, w_down, b_down4, layer):
    mrows = act.shape[0]
    nt = mrows // TM_MOE
    return pl.pallas_call(
        _down_kernel,
        out_shape=jax.ShapeDtypeStruct((mrows, D_MODEL), F32),
        grid_spec=pltpu.PrefetchScalarGridSpec(
            num_scalar_prefetch=2,
            grid=(D_MODEL // TN_DOWN, nt),
            in_specs=[
                pl.BlockSpec((TM_MOE, D_EXPERT), lambda n, m, te, nu: (jnp.minimum(m, nu[0] - 1), 0)),
                pl.BlockSpec((None, None, D_EXPERT, TN_DOWN), lambda n, m, te, nu: (layer, te[m], 0, n)),
                pl.BlockSpec((None, None, 1, TN_DOWN), lambda n, m, te, nu: (layer, te[m], 0, n)),
            ],
            out_specs=pl.BlockSpec((TM_MOE, TN_DOWN), lambda n, m, te, nu: (m, n)),
            scratch_shapes=[pltpu.VMEM((D_EXPERT, TN_DOWN), BF16)],
        ),
        compiler_params=_cparams(("arbitrary", "arbitrary")),
        name="moe_down",
    )(tile_expert, n_used, act, w_down, b_down4)


def _combine_kernel(pos_ref, ys_ref, tw_ref, x_ref, mod_ref, gain_ref, o_ref, buf_ref, sem):
    i = pl.program_id(0)
    n = pl.num_programs(0)
    tm = x_ref.shape[0]
    t_total = n * tm
    rows = TOP_K * tm

    def issue(tile, slot):
        for kk in range(TOP_K):
            def body(r, carry, kk=kk):
                p = pos_ref[kk * t_total + tile * tm + r]
                pltpu.make_async_copy(ys_ref.at[pl.ds(p, 1)], buf_ref.at[slot, pl.ds(kk * tm + r, 1)],
                                      sem.at[slot]).start()
                return carry
            lax.fori_loop(0, tm, body, 0, unroll=8)

    @pl.when(i == 0)
    def _():
        issue(0, 0)

    @pl.when(i + 1 < n)
    def _():
        issue(i + 1, (i + 1) % 2)

    slot = i % 2
    pltpu.make_async_copy(ys_ref.at[pl.ds(0, rows)], buf_ref.at[slot], sem.at[slot]).wait()
    tw = tw_ref[...]
    y = tw[:, 0:1] * buf_ref[slot, 0:tm, :]
    for kk in range(1, TOP_K):
        y += tw[:, kk:kk + 1] * buf_ref[slot, kk * tm:(kk + 1) * tm, :]
    m = mod_ref[...]
    o_ref[...] = x_ref[...] + m[5:6] * (_rms(y) * gain_ref[...])


def _combine(pos_t, ys, top_w, x2, mod6, gain, seq):
    t, d = x2.shape
    per_b = seq // TM_CMB
    gain = gain.reshape(1, d)
    return pl.pallas_call(
        _combine_kernel,
        out_shape=jax.ShapeDtypeStruct((t, d), F32),
        grid_spec=pltpu.PrefetchScalarGridSpec(
            num_scalar_prefetch=1,
            grid=(t // TM_CMB,),
            in_specs=[
                pl.BlockSpec(memory_space=pl.ANY),
                pl.BlockSpec((TM_CMB, LANES), lambda i, pos: (i, 0)),
                pl.BlockSpec((TM_CMB, d), lambda i, pos: (i, 0)),
                pl.BlockSpec((None, N_MOD, d), lambda i, pos: (i // per_b, 0, 0)),
                pl.BlockSpec((1, d), lambda i, pos: (0, 0)),
            ],
            out_specs=pl.BlockSpec((TM_CMB, d), lambda i, pos: (i, 0)),
            scratch_shapes=[pltpu.VMEM((2, TOP_K * TM_CMB, d), F32), pltpu.SemaphoreType.DMA((2,))],
        ),
        compiler_params=_cparams(("arbitrary",)),
        name="moe_combine",
    )(pos_t, ys, top_w, x2, mod6, gain)


def kernel(x, c, positions, ada_w, ada_b, mix_pre_gain, mix_post_gain, ffn_pre_gain, ffn_post_gain,
           w_in, w_out, ssm_lam_re, ssm_lam_im, ssm_log_step, ssm_b_re, ssm_b_im, ssm_c_re, ssm_c_im,
           ssm_d, ssm_glu_w, ssm_glu_b, gla_wa, gla_ba, gla_norm_gain, ret_norm_gain,
           router_w, router_b, exp_w_gu, exp_b_gu, exp_w_down, exp_b_down):
    nbatch, seq, d = x.shape
    t = nbatch * seq
    depth = ada_w.shape[0]
    n_tiles = (t * TOP_K) // TM_MOE + N_EXPERTS

    x2 = x.reshape(t, d)
    pos = positions.reshape(t, 1)
    half = RET_DK // 2
    inv = jnp.power(ROPE_BASE, -jnp.arange(half, dtype=F32) / half)
    inv = jnp.concatenate([inv, inv]).reshape(1, RET_DK)
    mod = _ada(c, ada_w, ada_b)

    rw =jnp.pad(router_w, ((0, 0), (0, 0), (0, LANES - N_EXPERTS)))
    rw_hi = rw.astype(BF16)
    rw_lo = (rw - rw_hi.astype(F32)).astype(BF16)
    rb = jnp.pad(router_b, ((0, 0), (0, LANES - N_EXPERTS)), constant_values=-1e30)
    b_gu4 = exp_b_gu.reshape(depth, N_EXPERTS, 1, 2 * D_EXPERT)
    b_down4 = exp_b_down.reshape(depth, N_EXPERTS, 1, d)
    glu_w = ssm_glu_w.astype(BF16)
    w_in_t = jnp.swapaxes(w_in, 1, 2)

    for l in range(depth):
        mod6 = mod[l].reshape(nbatch, N_MOD, d)
        h = _prenorm(x2, mix_pre_gain[l], mod6, seq, shift_i=0, scale_i=1)
        u = _mm(h, w_in_t, l, OFF_U, SSM_WIDTH)
        gqk = _mm(h, w_in_t, l, OFF_GQK, 2 * GLA_QK)
        gv = _mm(h, w_in_t, l, OFF_GV, GLA_WIDTH)
        gg = _mm(h, w_in_t, l, OFF_GG, GLA_WIDTH)
        ga = _mm(h, w_in_t, l, OFF_GA, LANES, tn=LANES)
        rqk = _mm_shift(h, w_in_t, l, OFF_TAIL, 2 * RET_QK)
        rv = _mm_shift(h, w_in_t, l, OFF_TAIL + 2 * RET_QK, RET_WIDTH)
        rg = _mm_shift(h, w_in_t, l, OFF_TAIL + 2 * RET_QK + RET_WIDTH, RET_WIDTH)

        b_mat, c_mat, lam_tab = _ssm_params(ssm_lam_re[l], ssm_lam_im[l], ssm_log_step[l], ssm_b_re[l],
                                            ssm_b_im[l], ssm_c_re[l], ssm_c_im[l])
        y_ssm = _ssm(u, b_mat, c_mat, lam_tab, ssm_d[l], glu_w[l], ssm_glu_b[l], nbatch)
        y_gla = _gla(gqk, gv, gg, ga, gla_wa[l], gla_ba[l], gla_norm_gain[l], nbatch)
        y_ret = _ret(rqk, rv, rg, pos, inv, ret_norm_gain[l], nbatch)
        y = _mm_out(y_ssm, y_gla, y_ret, w_out, l)

        x2, h2, top_idx, top_w = _mix_post(y, x2, mod6, mix_post_gain[l], ffn_pre_gain[l],
                                           rw_hi[l], rw_lo[l], rb[l:l + 1], seq)
        row_src, tile_expert, n_used, pos_t = _route(top_idx[:, :TOP_K], n_tiles)
        xs = _gather_rows(row_src, h2)
        act = _expert_gu(tile_expert, n_used, xs, exp_w_gu, b_gu4, l)
        ys = _expert_down(tile_expert, n_used, act, exp_w_down, b_down4, l)
        x2 = _combine(pos_t, ys, top_w, x2, mod6, ffn_post_gain[l], seq)
    return x2.reshape(nbatch, seq, d)
```

```python
import functools
import math

import jax
import jax.numpy as jnp
from jax import lax
from jax.experimental import pallas as pl
from jax.experimental.pallas import tpu as pltpu

F32 = jnp.float32
BF16 = jnp.bfloat16
I32 = jnp.int32

D_MODEL = 4096
N_MOD = 6
SSM_WIDTH = 1024
SSM_GROUP = 16
SSM_GROUPS = 64
SSM_STATE = 64
SSM_LANES = SSM_GROUPS * SSM_STATE
GLA_WIDTH = 1536
GLA_HEADS = 4
GLA_DV = 384
GLA_DK = 192
GLA_QK = 768
GLA_RANK = 16
GLA_TAU = 16.0
GLA_CHUNK = 64
RET_WIDTH = 1536
RET_HEADS = 6
RET_DV = 256
RET_DK = 128
RET_QK = 768
ROPE_BASE = 10000.0
N_EXPERTS = 32
TOP_K = 4
D_EXPERT = 768
SWIGLU_LIMIT = 7.0
SWIGLU_ALPHA = 1.702
EPS = 1e-6
OFF_U = 0
OFF_GQK = 1024
OFF_GV = 2560
OFF_GG = 4096
OFF_GA = 5632
OFF_TAIL = 5648

LANES = 128
SUBLANES = 8
VMEM_LIMIT = 52 * 1024 * 1024

TM_ROW = 256
TM_MM = 512
TN_MM = 512
TB_SSM = 256
SSM_LC = 512
SSM_GB = 16
TB_GLA = 256
TB_RET = 256
TM_MOE = 256
TN_GU = 384
TN_DOWN = 2048
TM_CMB = 128


def _cparams(sem):
    return pltpu.CompilerParams(dimension_semantics=sem, vmem_limit_bytes=VMEM_LIMIT)


def _dot(a, b):
    return jnp.dot(a, b, preferred_element_type=F32)


def _dot_nt(a, b):
    return lax.dot_general(a, b, (((1,), (1,)), ((), ())), preferred_element_type=F32)


def _dot_tn(a, b):
    return lax.dot_general(a, b, (((0,), (0,)), ((), ())), preferred_element_type=F32)


def _split_bf16(x):
    hi = x.astype(BF16)
    lo = (x - hi.astype(F32)).astype(BF16)
    return hi, lo


def _dot3(a, b):
    ah, al = _split_bf16(a)
    bh, bl = _split_bf16(b)
    return _dot(ah, bh) + _dot(al, bh) + _dot(ah, bl)


def _ada_kernel(c_ref, w_ref, b_ref, o_ref, cab_ref):
    nb = c_ref.shape[0]
    d, tn = w_ref.shape
    rk = 64

    @pl.when(jnp.logical_and(pl.program_id(0) == 0, pl.program_id(1) == 0))
    def _():
        for b in range(nb):
            cc = c_ref[b]
            cab_ref[b] = jnp.broadcast_to(cc * jax.nn.sigmoid(cc), (d, LANES))

    def body(i, accs):
        r0 = pl.multiple_of(i * rk, rk)
        wk = w_ref[pl.ds(r0, rk), :]
        out = []
        for b in range(nb):
            ca = cab_ref[b, pl.ds(r0, rk), :]
            prod = jnp.concatenate([wk[:, q * LANES:(q + 1) * LANES] * ca for q in range(tn // LANES)], axis=1)
            out.append(accs[b] + jnp.sum(prod.reshape(rk // SUBLANES, SUBLANES, tn), axis=0))
        return tuple(out)

    accs = lax.fori_loop(0, d // rk, body, tuple(jnp.zeros((SUBLANES, tn), F32) for _ in range(nb)))
    for b in range(nb):
        o_ref[b:b + 1, :] = jnp.sum(accs[b], axis=0, keepdims=True) + b_ref[...]


def _ada(c, ada_w, ada_b):
    n_layers, d, n = ada_w.shape
    nb = c.shape[0]
    tn = 512
    return pl.pallas_call(
        _ada_kernel,
        out_shape=jax.ShapeDtypeStruct((n_layers, nb, n), F32),
        grid=(n_layers, n // tn),
        in_specs=[
            pl.BlockSpec((nb, d, 1), lambda l, j: (0, 0, 0)),
            pl.BlockSpec((None, d, tn), lambda l, j: (l, 0, j)),
            pl.BlockSpec((None, 1, tn), lambda l, j: (l, 0, j)),
        ],
        out_specs=pl.BlockSpec((None, nb, tn), lambda l, j: (l, 0, j)),
        scratch_shapes=[pltpu.VMEM((nb, d, LANES), F32)],
        compiler_params=_cparams(("arbitrary", "arbitrary")),
        name="ada_mod",
    )(c.reshape(nb, d, 1), ada_w, ada_b.reshape(n_layers, 1, n))


def _rms(x):
    return x * lax.rsqrt(jnp.mean(x * x, axis=-1, keepdims=True) + EPS)


def _prenorm_kernel(x_ref, g_ref, mod_ref, o_ref, *, shift_i, scale_i):
    m = mod_ref[...]
    h = _rms(x_ref[...]) * g_ref[...]
    o_ref[...] = (h * (1.0 + m[scale_i:scale_i + 1]) + m[shift_i:shift_i + 1]).astype(o_ref.dtype)


def _prenorm(x2, gain, mod6, seq, *, shift_i, scale_i):
    t, d = x2.shape
    per_b = seq // TM_ROW
    return pl.pallas_call(
        functools.partial(_prenorm_kernel, shift_i=shift_i, scale_i=scale_i),
        out_shape=jax.ShapeDtypeStruct((t, d), BF16),
        grid=(t // TM_ROW,),
        in_specs=[
            pl.BlockSpec((TM_ROW, d), lambda i: (i, 0)),
            pl.BlockSpec((1, d), lambda i: (0, 0)),
            pl.BlockSpec((None, N_MOD, d), lambda i: (i // per_b, 0, 0)),
        ],
        out_specs=pl.BlockSpec((TM_ROW, d), lambda i: (i, 0)),
        compiler_params=_cparams(("arbitrary",)),
        name="prenorm",
    )(x2, gain.reshape(1, d), mod6)


def _mm_kernel(x_ref, w_ref, o_ref, wbf_ref):
    @pl.when(pl.program_id(1) == 0)
    def _():
        wbf_ref[...] = w_ref[...].astype(BF16)

    o_ref[...] = _dot_nt(x_ref[...], wbf_ref[...]).astype(o_ref.dtype)


def _mm(x, w_t, layer, row_off, width, *, tn=TN_MM, out_dtype=F32):
    t, k = x.shape
    jb = row_off // tn
    return pl.pallas_call(
        _mm_kernel,
        out_shape=jax.ShapeDtypeStruct((t, width), out_dtype),
        grid=(width // tn, t // TM_MM),
        in_specs=[
            pl.BlockSpec((TM_MM, k), lambda j, i: (i, 0)),
            pl.BlockSpec((None, tn, k), lambda j, i: (layer, jb + j, 0)),
        ],
        out_specs=pl.BlockSpec((TM_MM, tn), lambda j, i: (i, j)),
        scratch_shapes=[pltpu.VMEM((tn, k), BF16)],
        compiler_params=_cparams(("arbitrary", "arbitrary")),
        name="proj_in",
    )(x, w_t)


def _mm_shift_kernel(x_ref, wa_ref, wb_ref, o_ref, wbf_ref, *, shift):
    @pl.when(pl.program_id(1) == 0)
    def _():
        tn = wa_ref.shape[0]
        wbf_ref[0:tn - shift, :] = wa_ref[shift:, :].astype(BF16)
        wbf_ref[tn - shift:, :] = wb_ref[0:shift, :].astype(BF16)

    o_ref[...] = _dot_nt(x_ref[...], wbf_ref[...]).astype(o_ref.dtype)


def _mm_shift(x, w_t, layer, row_off, width, *, out_dtype=F32):
    t, k = x.shape
    tn = TN_MM
    jb, shift = divmod(row_off, tn)
    assert shift % 16 == 0
    return pl.pallas_call(
        functools.partial(_mm_shift_kernel, shift=shift),
        out_shape=jax.ShapeDtypeStruct((t, width), out_dtype),
        grid=(width // tn, t // TM_MM),
        in_specs=[
            pl.BlockSpec((TM_MM, k), lambda j, i: (i, 0)),
            pl.BlockSpec((None, tn, k), lambda j, i: (layer, jb + j, 0)),
            pl.BlockSpec((None, tn, k), lambda j, i: (layer, jb + j + 1, 0)),
        ],
        out_specs=pl.BlockSpec((TM_MM, tn), lambda j, i: (i, j)),
        scratch_shapes=[pltpu.VMEM((tn, k), BF16)],
        compiler_params=_cparams(("arbitrary", "arbitrary")),
        name="proj_in_shift",
    )(x, w_t, w_t)


def _mm3_kernel(a_ref, b_ref, c_ref, w_ref, o_ref, wbf_ref):
    @pl.when(pl.program_id(1) == 0)
    def _():
        wbf_ref[...] = w_ref[...].astype(BF16)

    ka, kb = a_ref.shape[1], b_ref.shape[1]
    acc = _dot(a_ref[...], wbf_ref[0:ka, :])
    acc += _dot(b_ref[...], wbf_ref[ka:ka + kb, :])
    acc += _dot(c_ref[...], wbf_ref[ka + kb:, :])
    o_ref[...] = acc


def _mm_out(ya, yb, yc, w_full, layer):
    t = ya.shape[0]
    k, n = w_full.shape[1], w_full.shape[2]
    return pl.pallas_call(
        _mm3_kernel,
        out_shape=jax.ShapeDtypeStruct((t, n), F32),
        grid=(n // TN_MM, t // TM_MM),
        in_specs=[
            pl.BlockSpec((TM_MM, ya.shape[1]), lambda j, i: (i, 0)),
            pl.BlockSpec((TM_MM, yb.shape[1]), lambda j, i: (i, 0)),
            pl.BlockSpec((TM_MM, yc.shape[1]), lambda j, i: (i, 0)),
            pl.BlockSpec((None, k, TN_MM), lambda j, i: (layer, 0, j)),
        ],
        out_specs=pl.BlockSpec((TM_MM, TN_MM), lambda j, i: (i, j)),
        scratch_shapes=[pltpu.VMEM((k, TN_MM), BF16)],
        compiler_params=_cparams(("arbitrary", "arbitrary")),
        name="proj_out",
    )(ya, yb, yc, w_full)


def _ssm_params(lam_re, lam_im, log_step, b_re, b_im, c_re, c_im):
    step = jnp.exp(log_step)[:, None]
    ar, ai = lam_re * step, lam_im * step
    mag = jnp.exp(ar)
    lbr, lbi = mag * jnp.cos(ai), mag * jnp.sin(ai)
    den = lam_re * lam_re + lam_im * lam_im
    cfr = ((lbr - 1.0) * lam_re + lbi * lam_im) / den
    cfi = (lbi * lam_re - (lbr - 1.0) * lam_im) / den
    bbr = cfr[..., None] * b_re - cfi[..., None] * b_im
    bbi = cfr[..., None] * b_im + cfi[..., None] * b_re
    nj = SSM_GROUPS // SSM_GB
    eye = jnp.eye(SSM_GB, dtype=F32)

    def bmat(bb):
        bb = bb.reshape(nj, SSM_GB, SSM_STATE, SSM_GROUP)
        m = jnp.einsum("ab,jbph->jahbp", eye, bb)
        return m.reshape(nj, SSM_GB * SSM_GROUP, SSM_GB * SSM_STATE)

    def cmat(cc):
        cc = cc.reshape(nj, SSM_GB, SSM_GROUP, SSM_STATE)
        m = jnp.einsum("ab,jbhp->japbh", eye, cc)
        return m.reshape(nj, SSM_GB * SSM_STATE, SSM_GB * SSM_GROUP)

    b_mat = jnp.concatenate([bmat(bbr), bmat(bbi)], axis=-1).astype(BF16)
    c_mat = jnp.concatenate([cmat(c_re), -cmat(c_im)], axis=1).astype(BF16)

    def power(kk):
        m = jnp.exp(kk * ar[None])
        return ((m * jnp.cos(kk * ai[None])).reshape(-1, SSM_LANES),
                (m * jnp.sin(kk * ai[None])).reshape(-1, SSM_LANES))

    rows = jnp.arange(SUBLANES, dtype=F32)[:, None]
    shifts = []
    for s in (1, 2, 4):
        pr, pi = power(jnp.full((1, 1, 1), float(s), F32))
        keep = rows >= s
        shifts += [jnp.where(keep, pr, 0.0), jnp.where(keep, pi, 0.0)]
    pr, pi = power(jnp.arange(1, SUBLANES + 1, dtype=F32)[:, None, None])
    lam_tab = jnp.stack(shifts + [pr, pi])
    return b_mat, c_mat, lam_tab


def _ssm_kernel(u_ref, bm_ref, cm_ref, lt_ref, d_ref, gw_ref, gb_ref, o_ref, xr_ref, xi_ref, st_ref):
    tb = u_ref.shape[0]
    nj = bm_ref.shape[0]
    gw = SSM_GB * SSM_GROUP
    sw = SSM_GB * SSM_STATE

    @pl.when(pl.program_id(1) == 0)
    def _():
        st_ref[...] = jnp.zeros_like(st_ref)

    u = u_ref[...]
    ub = u.astype(BF16)
    for j in range(nj):
        bu = _dot(ub[:, j * gw:(j + 1) * gw], bm_ref[j])
        xr_ref[:, j * sw:(j + 1) * sw] = bu[:, :sw]
        xi_ref[:, j * sw:(j + 1) * sw] = bu[:, sw:]

    def row_body(r, carry):
        rs = pl.ds(pl.multiple_of(r * SUBLANES, SUBLANES), SUBLANES)
        for c in range(SSM_LANES // SSM_LC):
            sl = slice(c * SSM_LC, (c + 1) * SSM_LC)
            xr = xr_ref[rs, sl]
            xi = xi_ref[rs, sl]
            for k, s in enumerate((1, 2, 4)):
                lr, li = lt_ref[2 * k, :, sl], lt_ref[2 * k + 1, :, sl]
                sr = pltpu.roll(xr, s, axis=0)
                si = pltpu.roll(xi, s, axis=0)
                xr, xi = xr + (lr * sr - li * si), xi + (lr * si + li * sr)
            pr, pi = lt_ref[6, :, sl], lt_ref[7, :, sl]
            cr, ci = st_ref[0:1, sl], st_ref[1:2, sl]
            xr, xi = xr + (pr * cr - pi * ci), xi + (pr * ci + pi * cr)
            xr_ref[rs, sl] = xr
            xi_ref[rs, sl] = xi
            st_ref[0:1, sl] = xr[SUBLANES - 1:, :]
            st_ref[1:2, sl] = xi[SUBLANES - 1:, :]
        return carry

    lax.fori_loop(0, tb // SUBLANES, row_body, 0)

    ys = []
    for j in range(nj):
        xs = jnp.concatenate([xr_ref[:, j * sw:(j + 1) * sw], xi_ref[:, j * sw:(j + 1) * sw]], axis=-1)
        ys.append(_dot(xs.astype(BF16), cm_ref[j]))
    y = jnp.concatenate(ys, axis=-1) + d_ref[...] * u
    y = jax.nn.gelu(y)
    gate = _dot(y.astype(BF16), gw_ref[...]) + gb_ref[...]
    o_ref[...] = (y * jax.nn.sigmoid(gate)).astype(o_ref.dtype)


def _ssm(u, b_mat, c_mat, lam_tab, d_skip, glu_w, glu_b, nbatch):
    t, w = u.shape
    per_b = t // nbatch // TB_SSM
    full = lambda a: pl.BlockSpec(a.shape, lambda b, i: (0,) * a.ndim)
    d_row = d_skip.reshape(1, w)
    gb_row = glu_b.reshape(1, w)
    return pl.pallas_call(
        _ssm_kernel,
        out_shape=jax.ShapeDtypeStruct((t, w), BF16),
        grid=(nbatch, per_b),
        in_specs=[
            pl.BlockSpec((TB_SSM, w), lambda b, i: (b * per_b + i, 0)),
            full(b_mat), full(c_mat), full(lam_tab), full(d_row), full(glu_w), full(gb_row),
        ],
        out_specs=pl.BlockSpec((TB_SSM, w), lambda b, i: (b * per_b + i, 0)),
        scratch_shapes=[
            pltpu.VMEM((TB_SSM, SSM_LANES), F32),
            pltpu.VMEM((TB_SSM, SSM_LANES), F32),
            pltpu.VMEM((2, SSM_LANES), F32),
        ],
        compiler_params=_cparams(("arbitrary", "arbitrary")),
        name="s5_mixer",
    )(u, b_mat, c_mat, lam_tab, d_row, glu_w, gb_row)


def _log_sigmoid(z):
    return jnp.minimum(z, 0.0) - jnp.log1p(jnp.exp(-jnp.abs(z)))


def _silu(x):
    return x * jax.nn.sigmoid(x)


def _gla_kernel(qk_ref, v_ref, g_ref, a_ref, wa_ref, ba_ref, gain_ref, o_ref, st_ref):
    tb = qk_ref.shape[0]
    ck = GLA_CHUNK

    @pl.when(pl.program_id(1) == 0)
    def _():
        st_ref[...] = jnp.zeros_like(st_ref)

    log_a = _log_sigmoid(_dot3(a_ref[:, :GLA_RANK], wa_ref[...]) + ba_ref[...]) * (1.0 / GLA_TAU)
    ri = lax.broadcasted_iota(I32, (ck, ck), 0)
    ci = lax.broadcasted_iota(I32, (ck, ck), 1)
    causal = ri >= ci
    tri = jnp.where(causal, 1.0, 0.0).astype(BF16)
    for c in range(tb // ck):
        rows = slice(c * ck, (c + 1) * ck)
        la_hi, la_lo = _split_bf16(log_a[rows])
        dec = _dot(tri, la_hi) + _dot(tri, la_lo)
        for h in range(GLA_HEADS):
            kcol = slice(h * GLA_DK, (h + 1) * GLA_DK)
            vcol = slice(h * GLA_DV, (h + 1) * GLA_DV)
            b = dec[:, kcol]
            b_last = b[ck - 1:ck, :]
            q = qk_ref[rows, kcol] * (GLA_DK ** -0.5)
            k = qk_ref[rows, GLA_QK + h * GLA_DK:GLA_QK + (h + 1) * GLA_DK]
            v = v_ref[rows, vcol].astype(BF16)
            q_dec = (q * jnp.exp(b)).astype(BF16)
            k_intra = (k * jnp.exp(-b)).astype(BF16)
            k_state = (k * jnp.exp(b_last - b)).astype(BF16)
            scores = jnp.where(causal, _dot_nt(q_dec, k_intra), 0.0)
            st = st_ref[h]
            o = _dot(scores.astype(BF16), v) + _dot_nt(q_dec, st.astype(BF16))
            st_ref[h] = st * jnp.exp(b_last) + _dot_tn(v, k_state)
            o = _rms(o) * gain_ref[:, vcol] * _silu(g_ref[rows, vcol])
            o_ref[rows, vcol] = o.astype(o_ref.dtype)


def _gla(gqk, gv, gg, ga, wa, ba, gain, nbatch):
    t = gqk.shape[0]
    per_b = t // nbatch // TB_GLA
    row = lambda w: pl.BlockSpec((TB_GLA, w), lambda b, i: (b * per_b + i, 0))
    full = lambda a: pl.BlockSpec(a.shape, lambda b, i: (0,) * a.ndim)
    ba_row = ba.reshape(1, GLA_QK)
    gain_row = gain.reshape(1, GLA_WIDTH)
    return pl.pallas_call(
        _gla_kernel,
        out_shape=jax.ShapeDtypeStruct((t, GLA_WIDTH), BF16),
        grid=(nbatch, per_b),
        in_specs=[row(2 * GLA_QK), row(GLA_WIDTH), row(GLA_WIDTH), row(LANES),
                  full(wa), full(ba_row), full(gain_row)],
        out_specs=row(GLA_WIDTH),
        scratch_shapes=[pltpu.VMEM((GLA_HEADS, GLA_DV, GLA_DK), F32)],
        compiler_params=_cparams(("arbitrary", "arbitrary")),
        name="gla_mixer",
    )(gqk, gv, gg, ga, wa, ba_row, gain_row)


def _ret_kernel(qk_ref, v_ref, g_ref, pos_ref, inv_ref, gain_ref, o_ref, st_ref):
    tb = qk_ref.shape[0]

    @pl.when(pl.program_id(1) == 0)
    def _():
        st_ref[...] = jnp.zeros_like(st_ref)

    ang = pos_ref[...].astype(F32) * inv_ref[...]
    lane = lax.broadcasted_iota(I32, (1, RET_DK), 1)
    cos = jnp.cos(ang)
    sin = jnp.sin(ang) * jnp.where(lane < RET_DK // 2, -1.0, 1.0)
    ri = lax.broadcasted_iota(I32, (tb, tb), 0)
    ci = lax.broadcasted_iota(I32, (tb, tb), 1)
    rel = (ri - ci).astype(F32)
    idx = lax.broadcasted_iota(I32, (tb, 1), 0).astype(F32)
    for h in range(RET_HEADS):
        log_gamma = math.log1p(-(2.0 ** (-5.0 - h)))
        kcol = slice(h * RET_DK, (h + 1) * RET_DK)
        vcol = slice(h * RET_DV, (h + 1) * RET_DV)
        q = qk_ref[:, kcol]
        k = qk_ref[:, RET_QK + h * RET_DK:RET_QK + (h + 1) * RET_DK]
        q = (q * cos + pltpu.roll(q, RET_DK // 2, axis=1) * sin) * (RET_DK ** -0.5)
        k = k * cos + pltpu.roll(k, RET_DK // 2, axis=1) * sin
        v = v_ref[:, vcol].astype(BF16)
        decay = jnp.where(rel >= 0, jnp.exp(jnp.maximum(rel, 0.0) * log_gamma), 0.0)
        scores = _dot_nt(q.astype(BF16), k.astype(BF16)) * decay
        q_w = jnp.exp((idx + 1.0) * log_gamma)
        k_w = jnp.exp((tb - 1.0 - idx) * log_gamma)
        st = st_ref[h]
        o = _dot(scores.astype(BF16), v) + _dot((q * q_w).astype(BF16), st.astype(BF16))
        st_ref[h] = st * math.exp(tb * log_gamma) + _dot_tn((k * k_w).astype(BF16), v)
        o = _rms(o) * gain_ref[:, vcol] * _silu(g_ref[:, vcol])
        o_ref[:, vcol] = o.astype(o_ref.dtype)


def _ret(rqk, rv, rg, pos, inv, gain, nbatch):
    t = rqk.shape[0]
    per_b = t // nbatch // TB_RET
    row = lambda w: pl.BlockSpec((TB_RET, w), lambda b, i: (b * per_b + i, 0))
    full = lambda a: pl.BlockSpec(a.shape, lambda b, i: (0,) * a.ndim)
    gain_row = gain.reshape(1, RET_WIDTH)
    return pl.pallas_call(
        _ret_kernel,
        out_shape=jax.ShapeDtypeStruct((t, RET_WIDTH), BF16),
        grid=(nbatch, per_b),
        in_specs=[row(2 * RET_QK), row(RET_WIDTH), row(RET_WIDTH), row(1), full(inv), full(gain_row)],
        out_specs=row(RET_WIDTH),
        scratch_shapes=[pltpu.VMEM((RET_HEADS, RET_DK, RET_DV), F32)],
        compiler_params=_cparams(("arbitrary", "arbitrary")),
        name="ret_mixer",
    )(rqk, rv, rg, pos, inv, gain_row)


def _mix_post_kernel(y_ref, x_ref, mod_ref, gpost_ref, gpre_ref, rwh_ref, rwl_ref, rb_ref,
                     xo_ref, ho_ref, ti_ref, tw_ref):
    m = mod_ref[...]
    x = x_ref[...] + m[2:3] * (_rms(y_ref[...]) * gpost_ref[...])
    xo_ref[...] = x
    h = _rms(x) * gpre_ref[...] * (1.0 + m[4:5]) + m[3:4]
    ho_ref[...] = h
    hh, hl = _split_bf16(h)
    logits = _dot(hh, rwh_ref[...]) + _dot(hl, rwh_ref[...]) + _dot(hh, rwl_ref[...]) + rb_ref[...]
    lane = lax.broadcasted_iota(I32, logits.shape, 1)
    vals, idxs = [], []
    for _ in range(TOP_K):
        mx = jnp.max(logits, axis=-1, keepdims=True)
        ix = jnp.min(jnp.where(logits == mx, lane, LANES), axis=-1, keepdims=True)
        vals.append(mx)
        idxs.append(ix)
        logits = jnp.where(lane == ix, -jnp.inf, logits)
    es = [jnp.exp(v - vals[0]) for v in vals]
    inv_sum = 1.0 / (es[0] + es[1] + es[2] + es[3])
    ti = jnp.zeros(lane.shape, I32)
    tw = jnp.zeros(lane.shape, F32)
    for kk in range(TOP_K):
        ti = jnp.where(lane == kk, idxs[kk], ti)
        tw = jnp.where(lane == kk, es[kk] * inv_sum, tw)
    ti_ref[...] = ti
    tw_ref[...] = tw


def _mix_post(y, x2, mod6, gpost, gpre, rw_hi, rw_lo, rb, seq):
    t, d = x2.shape
    per_b = seq // TM_ROW
    row = lambda w: pl.BlockSpec((TM_ROW, w), lambda i: (i, 0))
    full = lambda a: pl.BlockSpec(a.shape, lambda i: (0,) * a.ndim)
    gpost, gpre = gpost.reshape(1, d), gpre.reshape(1, d)
    return pl.pallas_call(
        _mix_post_kernel,
        out_shape=(jax.ShapeDtypeStruct((t, d), F32), jax.ShapeDtypeStruct((t, d), F32),
                   jax.ShapeDtypeStruct((t, LANES), I32), jax.ShapeDtypeStruct((t, LANES), F32)),
        grid=(t // TM_ROW,),
        in_specs=[row(d), row(d), pl.BlockSpec((None, N_MOD, d), lambda i: (i // per_b, 0, 0)),
                  full(gpost), full(gpre), full(rw_hi), full(rw_lo), full(rb)],
        out_specs=(row(d), row(d), row(LANES), row(LANES)),
        compiler_params=_cparams(("arbitrary",)),
        name="mix_post_router",
    )(y, x2, mod6, gpost, gpre, rw_hi, rw_lo, rb)


def _route(top_idx, n_tiles):
    t = top_idx.shape[0]
    e_flat = top_idx.reshape(-1)
    onehot = e_flat[:, None] == jnp.arange(N_EXPERTS, dtype=I32)[None, :]
    blk = 256
    nblk = (t * TOP_K) // blk
    oh3 = onehot.astype(BF16).reshape(nblk, blk, N_EXPERTS)
    within = jnp.einsum("ij,bjk->bik", jnp.tril(jnp.ones((blk, blk), BF16)), oh3, preferred_element_type=F32)
    tot = within[:, -1, :]
    before = jnp.dot(jnp.tril(jnp.ones((nblk, nblk), BF16), -1), tot.astype(BF16), preferred_element_type=F32)
    csum = (within + before[:, None, :]).reshape(t * TOP_K, N_EXPERTS)
    counts = (before[-1] + tot[-1]).astype(I32)
    padded = ((counts + TM_MOE - 1) // TM_MOE) * TM_MOE
    ends = jnp.cumsum(padded)
    starts = ends - padded
    dest = jnp.sum(jnp.where(onehot, csum - 1.0 + starts.astype(F32)[None, :], 0.0), axis=1).astype(I32)
    row_src = (jnp.arange(n_tiles * TM_MOE, dtype=I32) % t).at[dest].set(jnp.arange(t * TOP_K, dtype=I32) // TOP_K)
    n_used = ends[-1] // TM_MOE
    tile_ids = jnp.minimum(jnp.arange(n_tiles, dtype=I32), n_used - 1)
    tile_expert = jnp.sum(ends[None, :] <= (tile_ids * TM_MOE)[:, None], axis=1).astype(I32)
    pos_t = dest.reshape(t, TOP_K).T.reshape(-1)
    return row_src, tile_expert, n_used.reshape(1).astype(I32), pos_t


def _new_weights(te_ref, m):
    prev = te_ref[jnp.maximum(m - 1, 0)]
    return jnp.logical_or(m == 0, te_ref[m] != prev)


def _gu_kernel(te_ref, nu_ref, src_ref, h_ref, wg_ref, wu_ref, bg_ref, bu_ref, o_ref,
               wgb_ref, wub_ref, xa_ref, xb_ref, sem, *, unrolled_issue):
    m = pl.program_id(1)
    nt = pl.num_programs(1)
    step = pl.program_id(0) * nt + m
    n_steps = pl.num_programs(0) * nt
    next_tile = jnp.where(m + 1 < nt, m + 1, 0)
    bufs = (xa_ref, xb_ref)

    def row_copy(src_row, slot, r):
        return pltpu.make_async_copy(h_ref.at[pl.ds(src_row, 1)], bufs[slot].at[pl.ds(r, 1)], sem.at[slot])

    def issue(tile, slot, unrolled):
        base = tile * TM_MOE
        if unrolled:
            for r in range(TM_MOE):
                row_copy(src_ref[base + r], slot, r).start()
        else:
            def body(r, carry):
                row_copy(src_ref[base + r], slot, r).start()
                return carry
            lax.fori_loop(0, TM_MOE, body, 0, unroll=8)

    def wait(slot):
        pltpu.make_async_copy(h_ref.at[pl.ds(0, TM_MOE)], bufs[slot], sem.at[slot]).wait()

    @pl.when(step == 0)
    def _():
        issue(0, 0, False)

    @pl.when(_new_weights(te_ref, m))
    def _():
        wgb_ref[...] = wg_ref[...].astype(BF16)
        wub_ref[...] = wu_ref[...].astype(BF16)

    valid = m < nu_ref[0]
    for slot in range(2):
        mine = step % 2 == slot

        @pl.when(jnp.logical_and(mine, valid))
        def _(slot=slot):
            wait(slot)
            issue(next_tile, 1 - slot, unrolled_issue)
            x = bufs[slot][...].astype(BF16)
            gate = _dot(x, wgb_ref[...]) + bg_ref[...]
            up = _dot(x, wub_ref[...]) + bu_ref[...]
            gate = jnp.minimum(gate, SWIGLU_LIMIT)
            up = jnp.clip(up, -SWIGLU_LIMIT, SWIGLU_LIMIT)
            o_ref[...] = ((up + 1.0) * gate * jax.nn.sigmoid(SWIGLU_ALPHA * gate)).astype(o_ref.dtype)

        @pl.when(jnp.logical_and(mine, jnp.logical_not(valid)))
        def _(slot=slot):
            wait(slot)
            issue(next_tile, 1 - slot, False)
            o_ref[...] = jnp.zeros_like(o_ref)

    for slot in range(2):
        @pl.when(jnp.logical_and(step == n_steps - 1, step % 2 == 1 - slot))
        def _(slot=slot):
            wait(slot)


def _expert_gu(tile_expert, n_used, row_src, h, w_gu, b_gu4, layer, *, unrolled_issue=True):
    mrows = row_src.shape[0]
    nt = mrows // TM_MOE
    nn = D_EXPERT // TN_GU
    wspec = lambda off: pl.BlockSpec((None, None, D_MODEL, TN_GU),
                                     lambda n, m, te, nu, src: (layer, te[m], 0, off + n))
    bspec = lambda off: pl.BlockSpec((None, None, 1, TN_GU), lambda n, m, te, nu, src: (layer, te[m], 0, off + n))
    return pl.pallas_call(
        functools.partial(_gu_kernel, unrolled_issue=unrolled_issue),
        out_shape=jax.ShapeDtypeStruct((mrows, D_EXPERT), BF16),
        grid_spec=pltpu.PrefetchScalarGridSpec(
            num_scalar_prefetch=3,
            grid=(nn, nt),
            in_specs=[pl.BlockSpec(memory_space=pl.ANY), wspec(0), wspec(nn), bspec(0), bspec(nn)],
            out_specs=pl.BlockSpec((TM_MOE, TN_GU), lambda n, m, te, nu, src: (m, n)),
            scratch_shapes=[pltpu.VMEM((D_MODEL, TN_GU), BF16), pltpu.VMEM((D_MODEL, TN_GU), BF16),
                            pltpu.VMEM((TM_MOE, D_MODEL), F32), pltpu.VMEM((TM_MOE, D_MODEL), F32),
                            pltpu.SemaphoreType.DMA((2,))],
        ),
        compiler_params=_cparams(("arbitrary", "arbitrary")),
        name="moe_gate_up",
    )(tile_expert, n_used, row_src, h, w_gu, w_gu, b_gu4, b_gu4)


def _down_kernel(te_ref, nu_ref, a_ref, w_ref, b_ref, o_ref, wb_ref):
    m = pl.program_id(1)

    @pl.when(_new_weights(te_ref, m))
    def _():
        wb_ref[...] = w_ref[...].astype(BF16)

    @pl.when(m < nu_ref[0])
    def _():
        o_ref[...] = _dot(a_ref[...], wb_ref[...]) + b_ref[...]

    @pl.when(m >= nu_ref[0])
    def _():
        o_ref[...] = jnp.zeros_like(o_ref)


def _expert_down(tile_expert, n_used, act, w_down, b_down4, layer):
    mrows = act.shape[0]
    nt = mrows // TM_MOE
    return pl.pallas_call(
        _down_kernel,
        out_shape=jax.ShapeDtypeStruct((mrows, D_MODEL), F32),
        grid_spec=pltpu.PrefetchScalarGridSpec(
            num_scalar_prefetch=2,
            grid=(D_MODEL // TN_DOWN, nt),
            in_specs=[
                pl.BlockSpec((TM_MOE, D_EXPERT), lambda n, m, te, nu: (jnp.minimum(m, nu[0] - 1), 0)),
                pl.BlockSpec((None, None, D_EXPERT, TN_DOWN), lambda n, m, te, nu: (layer, te[m], 0, n)),
                pl.BlockSpec((None, None, 1, TN_DOWN), lambda n, m, te, nu: (layer, te[m], 0, n)),
            ],
            out_specs=pl.BlockSpec((TM_MOE, TN_DOWN), lambda n, m, te, nu: (m, n)),
            scratch_shapes=[pltpu.VMEM((D_EXPERT, TN_DOWN), BF16)],
        ),
        compiler_params=_cparams(("arbitrary", "arbitrary")),
        name="moe_down",
    )(tile_expert, n_used, act, w_down, b_down4)


def _combine_kernel(pos_ref, ys_ref, tw_ref, x_ref, mod_ref, gain_ref, o_ref, buf_ref, sem):
    i = pl.program_id(0)
    n = pl.num_programs(0)
    tm = x_ref.shape[0]
    t_total = n * tm
    rows = TOP_K * tm

    def issue(tile, slot, unrolled):
        for kk in range(TOP_K):
            base = kk * t_total + tile * tm

            def start(r, kk=kk, base=base):
                pltpu.make_async_copy(ys_ref.at[pl.ds(pos_ref[base + r], 1)],
                                      buf_ref.at[slot, pl.ds(kk * tm + r, 1)], sem.at[slot]).start()

            if unrolled:
                for r in range(tm):
                    start(r)
            else:
                def body(r, carry, start=start):
                    start(r)
                    return carry
                lax.fori_loop(0, tm, body, 0, unroll=8)

    @pl.when(i == 0)
    def _():
        issue(0, 0, False)

    @pl.when(i + 1 < n)
    def _():
        issue(i + 1, (i + 1) % 2, True)

    slot = i % 2
    pltpu.make_async_copy(ys_ref.at[pl.ds(0, rows)], buf_ref.at[slot], sem.at[slot]).wait()
    tw = tw_ref[...]
    y = tw[:, 0:1] * buf_ref[slot, 0:tm, :]
    for kk in range(1, TOP_K):
        y += tw[:, kk:kk + 1] * buf_ref[slot, kk * tm:(kk + 1) * tm, :]
    m = mod_ref[...]
    o_ref[...] = x_ref[...] + m[5:6] * (_rms(y) * gain_ref[...])


def _combine(pos_t, ys, top_w, x2, mod6, gain, seq):
    t, d = x2.shape
    per_b = seq // TM_CMB
    gain = gain.reshape(1, d)
    return pl.pallas_call(
        _combine_kernel,
        out_shape=jax.ShapeDtypeStruct((t, d), F32),
        grid_spec=pltpu.PrefetchScalarGridSpec(
            num_scalar_prefetch=1,
            grid=(t // TM_CMB,),
            in_specs=[
                pl.BlockSpec(memory_space=pl.ANY),
                pl.BlockSpec((TM_CMB, LANES), lambda i, pos: (i, 0)),
                pl.BlockSpec((TM_CMB, d), lambda i, pos: (i, 0)),
                pl.BlockSpec((None, N_MOD, d), lambda i, pos: (i // per_b, 0, 0)),
                pl.BlockSpec((1, d), lambda i, pos: (0, 0)),
            ],
            out_specs=pl.BlockSpec((TM_CMB, d), lambda i, pos: (i, 0)),
            scratch_shapes=[pltpu.VMEM((2, TOP_K * TM_CMB, d), F32), pltpu.SemaphoreType.DMA((2,))],
        ),
        compiler_params=_cparams(("arbitrary",)),
        name="moe_combine",
    )(pos_t, ys, top_w, x2, mod6, gain)


def kernel(x, c, positions, ada_w, ada_b, mix_pre_gain, mix_post_gain, ffn_pre_gain, ffn_post_gain,
           w_in, w_out, ssm_lam_re, ssm_lam_im, ssm_log_step, ssm_b_re, ssm_b_im, ssm_c_re, ssm_c_im,
           ssm_d, ssm_glu_w, ssm_glu_b, gla_wa, gla_ba, gla_norm_gain, ret_norm_gain,
           router_w, router_b, exp_w_gu, exp_b_gu, exp_w_down, exp_b_down):
    nbatch, seq, d = x.shape
    t = nbatch * seq
    depth = ada_w.shape[0]
    n_tiles = (t * TOP_K) // TM_MOE + N_EXPERTS

    x2 = x.reshape(t, d)
    pos = positions.reshape(t, 1)
    half = RET_DK // 2
    inv = jnp.power(ROPE_BASE, -jnp.arange(half, dtype=F32) / half)
    inv = jnp.concatenate([inv, inv]).reshape(1, RET_DK)
    mod = _ada(c, ada_w, ada_b)

    rw =jnp.pad(router_w, ((0, 0), (0, 0), (0, LANES - N_EXPERTS)))
    rw_hi = rw.astype(BF16)
    rw_lo = (rw - rw_hi.astype(F32)).astype(BF16)
    rb = jnp.pad(router_b, ((0, 0), (0, LANES - N_EXPERTS)), constant_values=-1e30)
    b_gu4 = exp_b_gu.reshape(depth, N_EXPERTS, 1, 2 * D_EXPERT)
    b_down4 = exp_b_down.reshape(depth, N_EXPERTS, 1, d)
    glu_w = ssm_glu_w.astype(BF16)
    w_in_t = jnp.swapaxes(w_in, 1, 2)

    for l in range(depth):
        mod6 = mod[l].reshape(nbatch, N_MOD, d)
        h = _prenorm(x2, mix_pre_gain[l], mod6, seq, shift_i=0, scale_i=1)
        u = _mm(h, w_in_t, l, OFF_U, SSM_WIDTH)
        gqk = _mm(h, w_in_t, l, OFF_GQK, 2 * GLA_QK)
        gv = _mm(h, w_in_t, l, OFF_GV, GLA_WIDTH)
        gg = _mm(h, w_in_t, l, OFF_GG, GLA_WIDTH)
        ga = _mm(h, w_in_t, l, OFF_GA, LANES, tn=LANES)
        rqk = _mm_shift(h, w_in_t, l, OFF_TAIL, 2 * RET_QK)
        rv = _mm_shift(h, w_in_t, l, OFF_TAIL + 2 * RET_QK, RET_WIDTH)
        rg = _mm_shift(h, w_in_t, l, OFF_TAIL + 2 * RET_QK + RET_WIDTH, RET_WIDTH)

        b_mat, c_mat, lam_tab = _ssm_params(ssm_lam_re[l], ssm_lam_im[l], ssm_log_step[l], ssm_b_re[l],
                                            ssm_b_im[l], ssm_c_re[l], ssm_c_im[l])
        y_ssm = _ssm(u, b_mat, c_mat, lam_tab, ssm_d[l], glu_w[l], ssm_glu_b[l], nbatch)
        y_gla = _gla(gqk, gv, gg, ga, gla_wa[l], gla_ba[l], gla_norm_gain[l], nbatch)
        y_ret = _ret(rqk, rv, rg, pos, inv, ret_norm_gain[l], nbatch)
        y = _mm_out(y_ssm, y_gla, y_ret, w_out, l)

        x2, h2, top_idx, top_w = _mix_post(y, x2, mod6, mix_post_gain[l], ffn_pre_gain[l],
                                           rw_hi[l], rw_lo[l], rb[l:l + 1], seq)
        row_src, tile_expert, n_used, pos_t = _route(top_idx[:, :TOP_K], n_tiles)
        act = _expert_gu(tile_expert, n_used, row_src, h2, exp_w_gu, b_gu4, l)
        ys = _expert_down(tile_expert, n_used, act, exp_w_down, b_down4, l)
        x2 = _combine(pos_t, ys, top_w, x2, mod6, ffn_post_gain[l], seq)
    return x2.reshape(nbatch, seq, d)
```

```python
import functools
import math

import jax
import jax.numpy as jnp
from jax import lax
from jax.experimental import pallas as pl
from jax.experimental.pallas import tpu as pltpu

F32 = jnp.float32
BF16 = jnp.bfloat16
I32 = jnp.int32

D_MODEL = 4096
N_MOD = 6
SSM_WIDTH = 1024
SSM_GROUP = 16
SSM_GROUPS = 64
SSM_STATE = 64
SSM_LANES = SSM_GROUPS * SSM_STATE
GLA_WIDTH = 1536
GLA_HEADS = 4
GLA_DV = 384
GLA_DK = 192
GLA_QK = 768
GLA_RANK = 16
GLA_TAU = 16.0
GLA_CHUNK = 64
RET_WIDTH = 1536
RET_HEADS = 6
RET_DV = 256
RET_DK = 128
RET_QK = 768
ROPE_BASE = 10000.0
N_EXPERTS = 32
TOP_K = 4
D_EXPERT = 768
SWIGLU_LIMIT = 7.0
SWIGLU_ALPHA = 1.702
EPS = 1e-6
OFF_U = 0
OFF_GQK = 1024
OFF_GV = 2560
OFF_GG = 4096
OFF_GA = 5632
OFF_TAIL = 5648

LANES = 128
SUBLANES = 8
VMEM_LIMIT = 52 * 1024 * 1024

TM_ROW = 256
TM_MM = 1024
TN_MM = 512
TB_SSM = 256
SSM_LC = 512
SSM_GB = 16
TB_GLA = 256
TB_RET = 256
TM_MOE = 256
TN_GU = 384
TN_DOWN = 2048
TM_CMB = 128
ROUTE_STRIDE = 2053


def _cparams(sem):
    return pltpu.CompilerParams(dimension_semantics=sem, vmem_limit_bytes=VMEM_LIMIT)


def _dot(a, b):
    return jnp.dot(a, b, preferred_element_type=F32)


def _dot_nt(a, b):
    return lax.dot_general(a, b, (((1,), (1,)), ((), ())), preferred_element_type=F32)


def _dot_tn(a, b):
    return lax.dot_general(a, b, (((0,), (0,)), ((), ())), preferred_element_type=F32)


def _split_bf16(x):
    hi = x.astype(BF16)
    lo = (x - hi.astype(F32)).astype(BF16)
    return hi, lo


def _dot3(a, b):
    ah, al = _split_bf16(a)
    bh, bl = _split_bf16(b)
    return _dot(ah, bh) + _dot(al, bh) + _dot(ah, bl)


def _ada_kernel(c_ref, w_ref, b_ref, o_ref, cab_ref):
    nb = c_ref.shape[0]
    d, tn = w_ref.shape
    rk = 64

    @pl.when(jnp.logical_and(pl.program_id(0) == 0, pl.program_id(1) == 0))
    def _():
        for b in range(nb):
            cc = c_ref[b]
            cab_ref[b] = jnp.broadcast_to(cc * jax.nn.sigmoid(cc), (d, LANES))

    def body(i, accs):
        r0 = pl.multiple_of(i * rk, rk)
        wk = w_ref[pl.ds(r0, rk), :]
        out = []
        for b in range(nb):
            ca = cab_ref[b, pl.ds(r0, rk), :]
            prod = jnp.concatenate([wk[:, q * LANES:(q + 1) * LANES] * ca for q in range(tn // LANES)], axis=1)
            out.append(accs[b] + jnp.sum(prod.reshape(rk // SUBLANES, SUBLANES, tn), axis=0))
        return tuple(out)

    accs = lax.fori_loop(0, d // rk, body, tuple(jnp.zeros((SUBLANES, tn), F32) for _ in range(nb)))
    for b in range(nb):
        o_ref[b:b + 1, :] = jnp.sum(accs[b], axis=0, keepdims=True) + b_ref[...]


def _ada(c, ada_w, ada_b):
    n_layers, d, n = ada_w.shape
    nb = c.shape[0]
    tn = 512
    return pl.pallas_call(
        _ada_kernel,
        out_shape=jax.ShapeDtypeStruct((n_layers, nb, n), F32),
        grid=(n_layers, n // tn),
        in_specs=[
            pl.BlockSpec((nb, d, 1), lambda l, j: (0, 0, 0)),
            pl.BlockSpec((None, d, tn), lambda l, j: (l, 0, j)),
            pl.BlockSpec((None, 1, tn), lambda l, j: (l, 0, j)),
        ],
        out_specs=pl.BlockSpec((None, nb, tn), lambda l, j: (l, 0, j)),
        scratch_shapes=[pltpu.VMEM((nb, d, LANES), F32)],
        compiler_params=_cparams(("arbitrary", "arbitrary")),
        name="ada_mod",
    )(c.reshape(nb, d, 1), ada_w, ada_b.reshape(n_layers, 1, n))


def _rms(x):
    return x * lax.rsqrt(jnp.mean(x * x, axis=-1, keepdims=True) + EPS)


def _prenorm_kernel(x_ref, g_ref, mod_ref, o_ref, *, shift_i, scale_i):
    m = mod_ref[...]
    h = _rms(x_ref[...]) * g_ref[...]
    o_ref[...] = (h * (1.0 + m[scale_i:scale_i + 1]) + m[shift_i:shift_i + 1]).astype(o_ref.dtype)


def _prenorm(x2, gain, mod6, seq, *, shift_i, scale_i):
    t, d = x2.shape
    per_b = seq // TM_ROW
    return pl.pallas_call(
        functools.partial(_prenorm_kernel, shift_i=shift_i, scale_i=scale_i),
        out_shape=jax.ShapeDtypeStruct((t, d), BF16),
        grid=(t // TM_ROW,),
        in_specs=[
            pl.BlockSpec((TM_ROW, d), lambda i: (i, 0)),
            pl.BlockSpec((1, d), lambda i: (0, 0)),
            pl.BlockSpec((None, N_MOD, d), lambda i: (i // per_b, 0, 0)),
        ],
        out_specs=pl.BlockSpec((TM_ROW, d), lambda i: (i, 0)),
        compiler_params=_cparams(("arbitrary",)),
        name="prenorm",
    )(x2, gain.reshape(1, d), mod6)


def _mm_kernel(x_ref, w_ref, o_ref, wbf_ref):
    @pl.when(pl.program_id(1) == 0)
    def _():
        wbf_ref[...] = w_ref[...].astype(BF16)

    o_ref[...] = _dot_nt(x_ref[...], wbf_ref[...]).astype(o_ref.dtype)


def _mm(x, w_t, layer, row_off, width, *, tn=TN_MM, out_dtype=F32):
    t, k = x.shape
    jb = row_off // tn
    return pl.pallas_call(
        _mm_kernel,
        out_shape=jax.ShapeDtypeStruct((t, width), out_dtype),
        grid=(width // tn, t // TM_MM),
        in_specs=[
            pl.BlockSpec((TM_MM, k), lambda j, i: (i, 0)),
            pl.BlockSpec((None, tn, k), lambda j, i: (layer, jb + j, 0)),
        ],
        out_specs=pl.BlockSpec((TM_MM, tn), lambda j, i: (i, j)),
        scratch_shapes=[pltpu.VMEM((tn, k), BF16)],
        compiler_params=_cparams(("arbitrary", "arbitrary")),
        name="proj_in",
    )(x, w_t)


def _mm_shift_kernel(x_ref, wa_ref, wb_ref, o_ref, wbf_ref, *, shift):
    @pl.when(pl.program_id(1) == 0)
    def _():
        tn = wa_ref.shape[0]
        wbf_ref[0:tn - shift, :] = wa_ref[shift:, :].astype(BF16)
        wbf_ref[tn - shift:, :] = wb_ref[...].astype(BF16)

    o_ref[...] = _dot_nt(x_ref[...], wbf_ref[...]).astype(o_ref.dtype)


def _mm_shift(x, w_t, layer, row_off, width, *, out_dtype=F32):
    t, k = x.shape
    tn = TN_MM
    jb, shift = divmod(row_off, tn)
    assert shift % 16 == 0 and tn % shift == 0
    return pl.pallas_call(
        functools.partial(_mm_shift_kernel, shift=shift),
        out_shape=jax.ShapeDtypeStruct((t, width), out_dtype),
        grid=(width // tn, t // TM_MM),
        in_specs=[
            pl.BlockSpec((TM_MM, k), lambda j, i: (i, 0)),
            pl.BlockSpec((None, tn, k), lambda j, i: (layer, jb + j, 0)),
            pl.BlockSpec((None, shift, k), lambda j, i: (layer, (jb + j + 1) * (tn // shift), 0)),
        ],
        out_specs=pl.BlockSpec((TM_MM, tn), lambda j, i: (i, j)),
        scratch_shapes=[pltpu.VMEM((tn, k), BF16)],
        compiler_params=_cparams(("arbitrary", "arbitrary")),
        name="proj_in_shift",
    )(x, w_t, w_t)


def _mm3_kernel(a_ref, b_ref, c_ref, w_ref, o_ref, wbf_ref):
    @pl.when(pl.program_id(1) == 0)
    def _():
        wbf_ref[...] = w_ref[...].astype(BF16)

    ka, kb = a_ref.shape[1], b_ref.shape[1]
    acc = _dot(a_ref[...], wbf_ref[0:ka, :])
    acc += _dot(b_ref[...], wbf_ref[ka:ka + kb, :])
    acc += _dot(c_ref[...], wbf_ref[ka + kb:, :])
    o_ref[...] = acc


def _mm_out(ya, yb, yc, w_full, layer):
    t = ya.shape[0]
    k, n = w_full.shape[1], w_full.shape[2]
    return pl.pallas_call(
        _mm3_kernel,
        out_shape=jax.ShapeDtypeStruct((t, n), F32),
        grid=(n // TN_MM, t // TM_MM),
        in_specs=[
            pl.BlockSpec((TM_MM, ya.shape[1]), lambda j, i: (i, 0)),
            pl.BlockSpec((TM_MM, yb.shape[1]), lambda j, i: (i, 0)),
            pl.BlockSpec((TM_MM, yc.shape[1]), lambda j, i: (i, 0)),
            pl.BlockSpec((None, k, TN_MM), lambda j, i: (layer, 0, j)),
        ],
        out_specs=pl.BlockSpec((TM_MM, TN_MM), lambda j, i: (i, j)),
        scratch_shapes=[pltpu.VMEM((k, TN_MM), BF16)],
        compiler_params=_cparams(("arbitrary", "arbitrary")),
        name="proj_out",
    )(ya, yb, yc, w_full)


def _ssm_params(lam_re, lam_im, log_step, b_re, b_im, c_re, c_im):
    step = jnp.exp(log_step)[:, None]
    ar, ai = lam_re * step, lam_im * step
    mag = jnp.exp(ar)
    lbr, lbi = mag * jnp.cos(ai), mag * jnp.sin(ai)
    den = lam_re * lam_re + lam_im * lam_im
    cfr = ((lbr - 1.0) * lam_re + lbi * lam_im) / den
    cfi = (lbi * lam_re - (lbr - 1.0) * lam_im) / den
    bbr = cfr[..., None] * b_re - cfi[..., None] * b_im
    bbi = cfr[..., None] * b_im + cfi[..., None] * b_re
    nj = SSM_GROUPS // SSM_GB
    eye = jnp.eye(SSM_GB, dtype=F32)

    def bmat(bb):
        bb = bb.reshape(nj, SSM_GB, SSM_STATE, SSM_GROUP)
        m = jnp.einsum("ab,jbph->jahbp", eye, bb)
        return m.reshape(nj, SSM_GB * SSM_GROUP, SSM_GB * SSM_STATE)

    def cmat(cc):
        cc = cc.reshape(nj, SSM_GB, SSM_GROUP, SSM_STATE)
        m = jnp.einsum("ab,jbhp->japbh", eye, cc)
        return m.reshape(nj, SSM_GB * SSM_STATE, SSM_GB * SSM_GROUP)

    b_mat = jnp.concatenate([bmat(bbr), bmat(bbi)], axis=-1).astype(BF16)
    c_mat = jnp.concatenate([cmat(c_re), -cmat(c_im)], axis=1).astype(BF16)

    def power(kk):
        m = jnp.exp(kk * ar[None])
        return ((m * jnp.cos(kk * ai[None])).reshape(-1, SSM_LANES),
                (m * jnp.sin(kk * ai[None])).reshape(-1, SSM_LANES))

    rows = jnp.arange(SUBLANES, dtype=F32)[:, None]
    shifts = []
    for s in (1, 2, 4):
        pr, pi = power(jnp.full((1, 1, 1), float(s), F32))
        keep = rows >= s
        shifts += [jnp.where(keep, pr, 0.0), jnp.where(keep, pi, 0.0)]
    pr, pi = power(jnp.arange(1, SUBLANES + 1, dtype=F32)[:, None, None])
    lam_tab = jnp.stack(shifts + [pr, pi])
    return b_mat, c_mat, lam_tab


def _ssm_kernel(u_ref, bm_ref, cm_ref, lt_ref, d_ref, gw_ref, gb_ref, o_ref, xr_ref, xi_ref, st_ref):
    tb = u_ref.shape[0]
    nj = bm_ref.shape[0]
    gw = SSM_GB * SSM_GROUP
    sw = SSM_GB * SSM_STATE

    @pl.when(pl.program_id(1) == 0)
    def _():
        st_ref[...] = jnp.zeros_like(st_ref)

    u = u_ref[...]
    ub = u.astype(BF16)
    for j in range(nj):
        bu = _dot(ub[:, j * gw:(j + 1) * gw], bm_ref[j])
        xr_ref[:, j * sw:(j + 1) * sw] = bu[:, :sw]
        xi_ref[:, j * sw:(j + 1) * sw] = bu[:, sw:]

    def row_body(r, carry):
        rs = pl.ds(pl.multiple_of(r * SUBLANES, SUBLANES), SUBLANES)
        for c in range(SSM_LANES // SSM_LC):
            sl = slice(c * SSM_LC, (c + 1) * SSM_LC)
            xr = xr_ref[rs, sl]
            xi = xi_ref[rs, sl]
            for k, s in enumerate((1, 2, 4)):
                lr, li = lt_ref[2 * k, :, sl], lt_ref[2 * k + 1, :, sl]
                sr = pltpu.roll(xr, s, axis=0)
                si = pltpu.roll(xi, s, axis=0)
                xr, xi = xr + (lr * sr - li * si), xi + (lr * si + li * sr)
            pr, pi = lt_ref[6, :, sl], lt_ref[7, :, sl]
            cr, ci = st_ref[0:1, sl], st_ref[1:2, sl]
            xr, xi = xr + (pr * cr - pi * ci), xi + (pr * ci + pi * cr)
            xr_ref[rs, sl] = xr
            xi_ref[rs, sl] = xi
            st_ref[0:1, sl] = xr[SUBLANES - 1:, :]
            st_ref[1:2, sl] = xi[SUBLANES - 1:, :]
        return carry

    lax.fori_loop(0, tb // SUBLANES, row_body, 0)

    ys = []
    for j in range(nj):
        xs = jnp.concatenate([xr_ref[:, j * sw:(j + 1) * sw], xi_ref[:, j * sw:(j + 1) * sw]], axis=-1)
        ys.append(_dot(xs.astype(BF16), cm_ref[j]))
    y = jnp.concatenate(ys, axis=-1) + d_ref[...] * u
    y = jax.nn.gelu(y)
    gate = _dot(y.astype(BF16), gw_ref[...]) + gb_ref[...]
    o_ref[...] = (y * jax.nn.sigmoid(gate)).astype(o_ref.dtype)


def _ssm(u, b_mat, c_mat, lam_tab, d_skip, glu_w, glu_b, nbatch):
    t, w = u.shape
    per_b = t // nbatch // TB_SSM
    full = lambda a: pl.BlockSpec(a.shape, lambda b, i: (0,) * a.ndim)
    d_row = d_skip.reshape(1, w)
    gb_row = glu_b.reshape(1, w)
    return pl.pallas_call(
        _ssm_kernel,
        out_shape=jax.ShapeDtypeStruct((t, w), BF16),
        grid=(nbatch, per_b),
        in_specs=[
            pl.BlockSpec((TB_SSM, w), lambda b, i: (b * per_b + i, 0)),
            full(b_mat), full(c_mat), full(lam_tab), full(d_row), full(glu_w), full(gb_row),
        ],
        out_specs=pl.BlockSpec((TB_SSM, w), lambda b, i: (b * per_b + i, 0)),
        scratch_shapes=[
            pltpu.VMEM((TB_SSM, SSM_LANES), F32),
            pltpu.VMEM((TB_SSM, SSM_LANES), F32),
            pltpu.VMEM((2, SSM_LANES), F32),
        ],
        compiler_params=_cparams(("arbitrary", "arbitrary")),
        name="s5_mixer",
    )(u, b_mat, c_mat, lam_tab, d_row, glu_w, gb_row)


def _log_sigmoid(z):
    return jnp.minimum(z, 0.0) - jnp.log1p(jnp.exp(-jnp.abs(z)))


def _silu(x):
    return x * jax.nn.sigmoid(x)


def _gla_kernel(qk_ref, v_ref, g_ref, a_ref, wa_ref, ba_ref, gain_ref, o_ref, st_ref):
    tb = qk_ref.shape[0]
    ck = GLA_CHUNK

    @pl.when(pl.program_id(1) == 0)
    def _():
        st_ref[...] = jnp.zeros_like(st_ref)

    log_a = _log_sigmoid(_dot3(a_ref[:, :GLA_RANK], wa_ref[...]) + ba_ref[...]) * (1.0 / GLA_TAU)
    ri = lax.broadcasted_iota(I32, (ck, ck), 0)
    ci = lax.broadcasted_iota(I32, (ck, ck), 1)
    causal = ri >= ci
    tri = jnp.where(causal, 1.0, 0.0).astype(BF16)
    for c in range(tb // ck):
        rows = slice(c * ck, (c + 1) * ck)
        la_hi, la_lo = _split_bf16(log_a[rows])
        dec = _dot(tri, la_hi) + _dot(tri, la_lo)
        for h in range(GLA_HEADS):
            kcol = slice(h * GLA_DK, (h + 1) * GLA_DK)
            vcol = slice(h * GLA_DV, (h + 1) * GLA_DV)
            b = dec[:, kcol]
            b_last = b[ck - 1:ck, :]
            q = qk_ref[rows, kcol] * (GLA_DK ** -0.5)
            k = qk_ref[rows, GLA_QK + h * GLA_DK:GLA_QK + (h + 1) * GLA_DK]
            v = v_ref[rows, vcol].astype(BF16)
            q_dec = (q * jnp.exp(b)).astype(BF16)
            k_intra = (k * jnp.exp(-b)).astype(BF16)
            k_state = (k * jnp.exp(b_last - b)).astype(BF16)
            scores = jnp.where(causal, _dot_nt(q_dec, k_intra), 0.0)
            st = st_ref[h]
            o = _dot(scores.astype(BF16), v) + _dot_nt(q_dec, st.astype(BF16))
            st_ref[h] = st * jnp.exp(b_last) + _dot_tn(v, k_state)
            o = _rms(o) * gain_ref[:, vcol] * _silu(g_ref[rows, vcol])
            o_ref[rows, vcol] = o.astype(o_ref.dtype)


def _gla(gqk, gv, gg, ga, wa, ba, gain, nbatch):
    t = gqk.shape[0]
    per_b = t // nbatch // TB_GLA
    row = lambda w: pl.BlockSpec((TB_GLA, w), lambda b, i: (b * per_b + i, 0))
    full = lambda a: pl.BlockSpec(a.shape, lambda b, i: (0,) * a.ndim)
    ba_row = ba.reshape(1, GLA_QK)
    gain_row = gain.reshape(1, GLA_WIDTH)
    return pl.pallas_call(
        _gla_kernel,
        out_shape=jax.ShapeDtypeStruct((t, GLA_WIDTH), BF16),
        grid=(nbatch, per_b),
        in_specs=[row(2 * GLA_QK), row(GLA_WIDTH), row(GLA_WIDTH), row(LANES),
                  full(wa), full(ba_row), full(gain_row)],
        out_specs=row(GLA_WIDTH),
        scratch_shapes=[pltpu.VMEM((GLA_HEADS, GLA_DV, GLA_DK), F32)],
        compiler_params=_cparams(("arbitrary", "arbitrary")),
        name="gla_mixer",
    )(gqk, gv, gg, ga, wa, ba_row, gain_row)


def _ret_kernel(qk_ref, v_ref, g_ref, pos_ref, inv_ref, gain_ref, o_ref, st_ref):
    tb = qk_ref.shape[0]

    @pl.when(pl.program_id(1) == 0)
    def _():
        st_ref[...] = jnp.zeros_like(st_ref)

    ang = pos_ref[...].astype(F32) * inv_ref[...]
    lane = lax.broadcasted_iota(I32, (1, RET_DK), 1)
    cos = jnp.cos(ang)
    sin = jnp.sin(ang) * jnp.where(lane < RET_DK // 2, -1.0, 1.0)
    ri = lax.broadcasted_iota(I32, (tb, tb), 0)
    ci = lax.broadcasted_iota(I32, (tb, tb), 1)
    rel = (ri - ci).astype(F32)
    idx = lax.broadcasted_iota(I32, (tb, 1), 0).astype(F32)
    for h in range(RET_HEADS):
        log_gamma = math.log1p(-(2.0 ** (-5.0 - h)))
        kcol = slice(h * RET_DK, (h + 1) * RET_DK)
        vcol = slice(h * RET_DV, (h + 1) * RET_DV)
        q = qk_ref[:, kcol]
        k = qk_ref[:, RET_QK + h * RET_DK:RET_QK + (h + 1) * RET_DK]
        q = (q * cos + pltpu.roll(q, RET_DK // 2, axis=1) * sin) * (RET_DK ** -0.5)
        k = k * cos + pltpu.roll(k, RET_DK // 2, axis=1) * sin
        v = v_ref[:, vcol].astype(BF16)
        decay = jnp.where(rel >= 0, jnp.exp(jnp.maximum(rel, 0.0) * log_gamma), 0.0)
        scores = _dot_nt(q.astype(BF16), k.astype(BF16)) * decay
        q_w = jnp.exp((idx + 1.0) * log_gamma)
        k_w = jnp.exp((tb - 1.0 - idx) * log_gamma)
        st = st_ref[h]
        o = _dot(scores.astype(BF16), v) + _dot((q * q_w).astype(BF16), st.astype(BF16))
        st_ref[h] = st * math.exp(tb * log_gamma) + _dot_tn((k * k_w).astype(BF16), v)
        o = _rms(o) * gain_ref[:, vcol] * _silu(g_ref[:, vcol])
        o_ref[:, vcol] = o.astype(o_ref.dtype)


def _ret(rqk, rv, rg, pos, inv, gain, nbatch):
    t = rqk.shape[0]
    per_b = t // nbatch // TB_RET
    row = lambda w: pl.BlockSpec((TB_RET, w), lambda b, i: (b * per_b + i, 0))
    full = lambda a: pl.BlockSpec(a.shape, lambda b, i: (0,) * a.ndim)
    gain_row = gain.reshape(1, RET_WIDTH)
    return pl.pallas_call(
        _ret_kernel,
        out_shape=jax.ShapeDtypeStruct((t, RET_WIDTH), BF16),
        grid=(nbatch, per_b),
        in_specs=[row(2 * RET_QK), row(RET_WIDTH), row(RET_WIDTH), row(1), full(inv), full(gain_row)],
        out_specs=row(RET_WIDTH),
        scratch_shapes=[pltpu.VMEM((RET_HEADS, RET_DK, RET_DV), F32)],
        compiler_params=_cparams(("arbitrary", "arbitrary")),
        name="ret_mixer",
    )(rqk, rv, rg, pos, inv, gain_row)


def _mix_post_kernel(y_ref, x_ref, mod_ref, gpost_ref, gpre_ref, rwh_ref, rwl_ref, rb_ref,
                     xo_ref, ho_ref, ti_ref, tw_ref):
    m = mod_ref[...]
    x = x_ref[...] + m[2:3] * (_rms(y_ref[...]) * gpost_ref[...])
    xo_ref[...] = x
    h = _rms(x) * gpre_ref[...] * (1.0 + m[4:5]) + m[3:4]
    ho_ref[...] = h
    hh, hl = _split_bf16(h)
    logits = _dot(hh, rwh_ref[...]) + _dot(hl, rwh_ref[...]) + _dot(hh, rwl_ref[...]) + rb_ref[...]
    lane = lax.broadcasted_iota(I32, logits.shape, 1)
    vals, idxs = [], []
    for _ in range(TOP_K):
        mx = jnp.max(logits, axis=-1, keepdims=True)
        ix = jnp.min(jnp.where(logits == mx, lane, LANES), axis=-1, keepdims=True)
        vals.append(mx)
        idxs.append(ix)
        logits = jnp.where(lane == ix, -jnp.inf, logits)
    es = [jnp.exp(v - vals[0]) for v in vals]
    inv_sum = 1.0 / (es[0] + es[1] + es[2] + es[3])
    ti = jnp.zeros(lane.shape, I32)
    tw = jnp.zeros(lane.shape, F32)
    for kk in range(TOP_K):
        ti = jnp.where(lane == kk, idxs[kk], ti)
        tw = jnp.where(lane == kk, es[kk] * inv_sum, tw)
    ti_ref[...] = ti
    tw_ref[...] = tw


def _mix_post(y, x2, mod6, gpost, gpre, rw_hi, rw_lo, rb, seq):
    t, d = x2.shape
    per_b = seq // TM_ROW
    row = lambda w: pl.BlockSpec((TM_ROW, w), lambda i: (i, 0))
    full = lambda a: pl.BlockSpec(a.shape, lambda i: (0,) * a.ndim)
    gpost, gpre = gpost.reshape(1, d), gpre.reshape(1, d)
    return pl.pallas_call(
        _mix_post_kernel,
        out_shape=(jax.ShapeDtypeStruct((t, d), F32), jax.ShapeDtypeStruct((t, d), F32),
                   jax.ShapeDtypeStruct((t, LANES), I32), jax.ShapeDtypeStruct((t, LANES), F32)),
        grid=(t // TM_ROW,),
        in_specs=[row(d), row(d), pl.BlockSpec((None, N_MOD, d), lambda i: (i // per_b, 0, 0)),
                  full(gpost), full(gpre), full(rw_hi), full(rw_lo), full(rb)],
        out_specs=(row(d), row(d), row(LANES), row(LANES)),
        compiler_params=_cparams(("arbitrary",)),
        name="mix_post_router",
    )(y, x2, mod6, gpost, gpre, rw_hi, rw_lo, rb)


def _route(top_idx, n_tiles):
    t = top_idx.shape[0]
    assert math.gcd(ROUTE_STRIDE, t) == 1
    perm = (jnp.arange(t, dtype=I32) * ROUTE_STRIDE) % t
    inv_perm = (jnp.arange(t, dtype=I32) * pow(ROUTE_STRIDE, -1, t)) % t
    e_flat = top_idx[perm].reshape(-1)
    onehot =e_flat[:, None] == jnp.arange(N_EXPERTS, dtype=I32)[None, :]
    blk = 256
    nblk = (t * TOP_K) // blk
    oh3 = onehot.astype(BF16).reshape(nblk, blk, N_EXPERTS)
    within = jnp.einsum("ij,bjk->bik", jnp.tril(jnp.ones((blk, blk), BF16)), oh3, preferred_element_type=F32)
    tot = within[:, -1, :]
    before = jnp.dot(jnp.tril(jnp.ones((nblk, nblk), BF16), -1), tot.astype(BF16), preferred_element_type=F32)
    csum = (within + before[:, None, :]).reshape(t * TOP_K, N_EXPERTS)
    counts = (before[-1] + tot[-1]).astype(I32)
    padded = ((counts + TM_MOE - 1) // TM_MOE) * TM_MOE
    ends = jnp.cumsum(padded)
    starts = ends - padded
    dest = jnp.sum(jnp.where(onehot, csum - 1.0 + starts.astype(F32)[None, :], 0.0), axis=1).astype(I32)
    row_src = (jnp.arange(n_tiles * TM_MOE, dtype=I32) % t).at[dest].set(jnp.repeat(perm, TOP_K))
    n_used = ends[-1] // TM_MOE
    tile_ids = jnp.minimum(jnp.arange(n_tiles, dtype=I32), n_used - 1)
    tile_expert = jnp.sum(ends[None, :] <= (tile_ids * TM_MOE)[:, None], axis=1).astype(I32)
    pos_t = dest.reshape(t, TOP_K)[inv_perm].T.reshape(-1)
    return row_src, tile_expert, n_used.reshape(1).astype(I32), pos_t


def _new_weights(te_ref, m):
    prev = te_ref[jnp.maximum(m - 1, 0)]
    return jnp.logical_or(m == 0, te_ref[m] != prev)


def _gu_kernel(te_ref, nu_ref, src_ref, h_ref, wg_ref, wu_ref, bg_ref, bu_ref, o_ref,
               wgb_ref, wub_ref, xa_ref, xb_ref, sem, *, unrolled_issue):
    m = pl.program_id(1)
    nt = pl.num_programs(1)
    step = pl.program_id(0) * nt + m
    n_steps = pl.num_programs(0) * nt
    next_tile = jnp.where(m + 1 < nt, m + 1, 0)
    bufs = (xa_ref, xb_ref)

    def row_copy(src_row, slot, r):
        return pltpu.make_async_copy(h_ref.at[pl.ds(src_row, 1)], bufs[slot].at[pl.ds(r, 1)], sem.at[slot])

    def issue(tile, slot, unrolled):
        base = tile * TM_MOE
        if unrolled:
            for r in range(TM_MOE):
                row_copy(src_ref[base + r], slot, r).start()
        else:
            def body(r, carry):
                row_copy(src_ref[base + r], slot, r).start()
                return carry
            lax.fori_loop(0, TM_MOE, body, 0, unroll=8)

    def wait(slot):
        pltpu.make_async_copy(h_ref.at[pl.ds(0, TM_MOE)], bufs[slot], sem.at[slot]).wait()

    @pl.when(step == 0)
    def _():
        issue(0, 0, False)

    @pl.when(_new_weights(te_ref, m))
    def _():
        wgb_ref[...] = wg_ref[...].astype(BF16)
        wub_ref[...] = wu_ref[...].astype(BF16)

    valid = m < nu_ref[0]
    for slot in range(2):
        mine = step % 2 == slot

        @pl.when(jnp.logical_and(mine, valid))
        def _(slot=slot):
            wait(slot)
            issue(next_tile, 1 - slot, unrolled_issue)
            x = bufs[slot][...].astype(BF16)
            gate = _dot(x, wgb_ref[...]) + bg_ref[...]
            up = _dot(x, wub_ref[...]) + bu_ref[...]
            gate = jnp.minimum(gate, SWIGLU_LIMIT)
            up = jnp.clip(up, -SWIGLU_LIMIT, SWIGLU_LIMIT)
            o_ref[...] = ((up + 1.0) * gate * jax.nn.sigmoid(SWIGLU_ALPHA * gate)).astype(o_ref.dtype)

        @pl.when(jnp.logical_and(mine, jnp.logical_not(valid)))
        def _(slot=slot):
            wait(slot)
            issue(next_tile, 1 - slot, False)
            o_ref[...] = jnp.zeros_like(o_ref)

    for slot in range(2):
        @pl.when(jnp.logical_and(step == n_steps - 1, step % 2 == 1 - slot))
        def _(slot=slot):
            wait(slot)


def _expert_gu(tile_expert, n_used, row_src, h, w_gu, b_gu4, layer, *, unrolled_issue=True):
    mrows = row_src.shape[0]
    nt = mrows // TM_MOE
    nn = D_EXPERT // TN_GU
    wspec = lambda off: pl.BlockSpec((None, None, D_MODEL, TN_GU),
                                     lambda n, m, te, nu, src: (layer, te[m], 0, off + n))
    bspec = lambda off: pl.BlockSpec((None, None, 1, TN_GU), lambda n, m, te, nu, src: (layer, te[m], 0, off + n))
    return pl.pallas_call(
        functools.partial(_gu_kernel, unrolled_issue=unrolled_issue),
        out_shape=jax.ShapeDtypeStruct((mrows, D_EXPERT), BF16),
        grid_spec=pltpu.PrefetchScalarGridSpec(
            num_scalar_prefetch=3,
            grid=(nn, nt),
            in_specs=[pl.BlockSpec(memory_space=pl.ANY), wspec(0), wspec(nn), bspec(0), bspec(nn)],
            out_specs=pl.BlockSpec((TM_MOE, TN_GU), lambda n, m, te, nu, src: (m, n)),
            scratch_shapes=[pltpu.VMEM((D_MODEL, TN_GU), BF16), pltpu.VMEM((D_MODEL, TN_GU), BF16),
                            pltpu.VMEM((TM_MOE, D_MODEL), F32), pltpu.VMEM((TM_MOE, D_MODEL), F32),
                            pltpu.SemaphoreType.DMA((2,))],
        ),
        compiler_params=_cparams(("arbitrary", "arbitrary")),
        name="moe_gate_up",
    )(tile_expert, n_used, row_src, h, w_gu, w_gu, b_gu4, b_gu4)


def _down_kernel(te_ref, nu_ref, a_ref, w_ref, b_ref, o_ref, wb_ref):
    m = pl.program_id(1)

    @pl.when(_new_weights(te_ref, m))
    def _():
        wb_ref[...] = w_ref[...].astype(BF16)

    @pl.when(m < nu_ref[0])
    def _():
        o_ref[...] = _dot(a_ref[...], wb_ref[...]) + b_ref[...]

    @pl.when(m >= nu_ref[0])
    def _():
        o_ref[...] = jnp.zeros_like(o_ref)


def _expert_down(tile_expert, n_used, act, w_down, b_down4, layer):
    mrows = act.shape[0]
    nt = mrows // TM_MOE
    return pl.pallas_call(
        _down_kernel,
        out_shape=jax.ShapeDtypeStruct((mrows, D_MODEL), F32),
        grid_spec=pltpu.PrefetchScalarGridSpec(
            num_scalar_prefetch=2,
            grid=(D_MODEL // TN_DOWN, nt),
            in_specs=[
                pl.BlockSpec((TM_MOE, D_EXPERT), lambda n, m, te, nu: (jnp.minimum(m, nu[0] - 1), 0)),
                pl.BlockSpec((None, None, D_EXPERT, TN_DOWN), lambda n, m, te, nu: (layer, te[m], 0, n)),
                pl.BlockSpec((None, None, 1, TN_DOWN), lambda n, m, te, nu: (layer, te[m], 0, n)),
            ],
            out_specs=pl.BlockSpec((TM_MOE, TN_DOWN), lambda n, m, te, nu: (m, n)),
            scratch_shapes=[pltpu.VMEM((D_EXPERT, TN_DOWN), BF16)],
        ),
        compiler_params=_cparams(("arbitrary", "arbitrary")),
        name="moe_down",
    )(tile_expert, n_used, act, w_down, b_down4)


def _combine_kernel(pos_ref, ys_ref, tw_ref, x_ref, mod_ref, gain_ref, o_ref, buf_ref, sem):
    i = pl.program_id(0)
    n = pl.num_programs(0)
    tm = x_ref.shape[0]
    t_total = n * tm
    rows = TOP_K * tm

    def issue(tile, slot, unrolled):
        for kk in range(TOP_K):
            base = kk * t_total + tile * tm

            def start(r, kk=kk, base=base):
                pltpu.make_async_copy(ys_ref.at[pl.ds(pos_ref[base + r], 1)],
                                      buf_ref.at[slot, pl.ds(kk * tm + r, 1)], sem.at[slot]).start()

            if unrolled:
                for r in range(tm):
                    start(r)
            else:
                def body(r, carry, start=start):
                    start(r)
                    return carry
                lax.fori_loop(0, tm, body, 0, unroll=8)

    @pl.when(i == 0)
    def _():
        issue(0, 0, False)

    @pl.when(i + 1 < n)
    def _():
        issue(i + 1, (i + 1) % 2, True)

    slot = i % 2
    pltpu.make_async_copy(ys_ref.at[pl.ds(0, rows)], buf_ref.at[slot], sem.at[slot]).wait()
    tw = tw_ref[...]
    y = tw[:, 0:1] * buf_ref[slot, 0:tm, :]
    for kk in range(1, TOP_K):
        y += tw[:, kk:kk + 1] * buf_ref[slot, kk * tm:(kk + 1) * tm, :]
    m = mod_ref[...]
    o_ref[...] = x_ref[...] + m[5:6] * (_rms(y) * gain_ref[...])


def _combine(pos_t, ys, top_w, x2, mod6, gain, seq):
    t, d = x2.shape
    per_b = seq // TM_CMB
    gain = gain.reshape(1, d)
    return pl.pallas_call(
        _combine_kernel,
        out_shape=jax.ShapeDtypeStruct((t, d), F32),
        grid_spec=pltpu.PrefetchScalarGridSpec(
            num_scalar_prefetch=1,
            grid=(t // TM_CMB,),
            in_specs=[
                pl.BlockSpec(memory_space=pl.ANY),
                pl.BlockSpec((TM_CMB, LANES), lambda i, pos: (i, 0)),
                pl.BlockSpec((TM_CMB, d), lambda i, pos: (i, 0)),
                pl.BlockSpec((None, N_MOD, d), lambda i, pos: (i // per_b, 0, 0)),
                pl.BlockSpec((1, d), lambda i, pos: (0, 0)),
            ],
            out_specs=pl.BlockSpec((TM_CMB, d), lambda i, pos: (i, 0)),
            scratch_shapes=[pltpu.VMEM((2, TOP_K * TM_CMB, d), F32), pltpu.SemaphoreType.DMA((2,))],
        ),
        compiler_params=_cparams(("arbitrary",)),
        name="moe_combine",
    )(pos_t, ys, top_w, x2, mod6, gain)


def kernel(x, c, positions, ada_w, ada_b, mix_pre_gain, mix_post_gain, ffn_pre_gain, ffn_post_gain,
           w_in, w_out, ssm_lam_re, ssm_lam_im, ssm_log_step, ssm_b_re, ssm_b_im, ssm_c_re, ssm_c_im,
           ssm_d, ssm_glu_w, ssm_glu_b, gla_wa, gla_ba, gla_norm_gain, ret_norm_gain,
           router_w, router_b, exp_w_gu, exp_b_gu, exp_w_down, exp_b_down):
    nbatch, seq, d = x.shape
    t = nbatch * seq
    depth = ada_w.shape[0]
    n_tiles = (t * TOP_K) // TM_MOE + N_EXPERTS

    x2 = x.reshape(t, d)
    pos = positions.reshape(t, 1)
    half = RET_DK // 2
    inv = jnp.power(ROPE_BASE, -jnp.arange(half, dtype=F32) / half)
    inv = jnp.concatenate([inv, inv]).reshape(1, RET_DK)
    mod = _ada(c, ada_w, ada_b)

    rw =jnp.pad(router_w, ((0, 0), (0, 0), (0, LANES - N_EXPERTS)))
    rw_hi = rw.astype(BF16)
    rw_lo = (rw - rw_hi.astype(F32)).astype(BF16)
    rb = jnp.pad(router_b, ((0, 0), (0, LANES - N_EXPERTS)), constant_values=-1e30)
    b_gu4 = exp_b_gu.reshape(depth, N_EXPERTS, 1, 2 * D_EXPERT)
    b_down4 = exp_b_down.reshape(depth, N_EXPERTS, 1, d)
    glu_w = ssm_glu_w.astype(BF16)
    w_in_t = jnp.swapaxes(w_in, 1, 2)

    for l in range(depth):
        mod6 = mod[l].reshape(nbatch, N_MOD, d)
        h = _prenorm(x2, mix_pre_gain[l], mod6, seq, shift_i=0, scale_i=1)
        u = _mm(h, w_in_t, l, OFF_U, SSM_WIDTH)
        gqk = _mm(h, w_in_t, l, OFF_GQK, 2 * GLA_QK)
        gv = _mm(h, w_in_t, l, OFF_GV, GLA_WIDTH)
        gg = _mm(h, w_in_t, l, OFF_GG, GLA_WIDTH)
        ga = _mm(h, w_in_t, l, OFF_GA, LANES, tn=LANES)
        rqk = _mm_shift(h, w_in_t, l, OFF_TAIL, 2 * RET_QK)
        rv = _mm_shift(h, w_in_t, l, OFF_TAIL + 2 * RET_QK, RET_WIDTH)
        rg = _mm_shift(h, w_in_t, l, OFF_TAIL + 2 * RET_QK + RET_WIDTH, RET_WIDTH)

        b_mat, c_mat, lam_tab = _ssm_params(ssm_lam_re[l], ssm_lam_im[l], ssm_log_step[l], ssm_b_re[l],
                                            ssm_b_im[l], ssm_c_re[l], ssm_c_im[l])
        y_ssm = _ssm(u, b_mat, c_mat, lam_tab, ssm_d[l], glu_w[l], ssm_glu_b[l], nbatch)
        y_gla = _gla(gqk, gv, gg, ga, gla_wa[l], gla_ba[l], gla_norm_gain[l], nbatch)
        y_ret = _ret(rqk, rv, rg, pos, inv, ret_norm_gain[l], nbatch)
        y = _mm_out(y_ssm, y_gla, y_ret, w_out, l)

        x2, h2, top_idx, top_w = _mix_post(y, x2, mod6, mix_post_gain[l], ffn_pre_gain[l],
                                           rw_hi[l], rw_lo[l], rb[l:l + 1], seq)
        row_src, tile_expert, n_used, pos_t = _route(top_idx[:, :TOP_K], n_tiles)
        act = _expert_gu(tile_expert, n_used, row_src, h2, exp_w_gu, b_gu4, l)
        ys = _expert_down(tile_expert, n_used, act, exp_w_down, b_down4, l)
        x2 = _combine(pos_t, ys, top_w, x2, mod6, ffn_post_gain[l], seq)
    return x2.reshape(nbatch, seq, d)
```

```python
import functools
import math

import jax
import jax.numpy as jnp
from jax import lax
from jax.experimental import pallas as pl
from jax.experimental.pallas import tpu as pltpu

F32 = jnp.float32
BF16 = jnp.bfloat16
I32 = jnp.int32

D_MODEL = 4096
N_MOD = 6
SSM_WIDTH = 1024
SSM_GROUP = 16
SSM_GROUPS = 64
SSM_STATE = 64
SSM_LANES = SSM_GROUPS * SSM_STATE
GLA_WIDTH = 1536
GLA_HEADS = 4
GLA_DV = 384
GLA_DK = 192
GLA_QK = 768
GLA_RANK = 16
GLA_TAU = 16.0
GLA_CHUNK = 64
RET_WIDTH = 1536
RET_HEADS = 6
RET_DV = 256
RET_DK = 128
RET_QK = 768
ROPE_BASE = 10000.0
N_EXPERTS = 32
TOP_K = 4
D_EXPERT = 768
SWIGLU_LIMIT = 7.0
SWIGLU_ALPHA = 1.702
EPS = 1e-6
OFF_U = 0
OFF_GQK = 1024
OFF_GV = 2560
OFF_GG = 4096
OFF_GA = 5632
OFF_TAIL = 5648

LANES = 128
SUBLANES = 8
VMEM_LIMIT = 52 * 1024 * 1024

TM_ROW = 256
TM_MM = 1024
TN_MM = 512
TB_SSM = 256
SSM_LC = 512
SSM_GB = 16
TB_GLA = 256
TB_RET = 256
TM_MOE = 256
TN_GU = 384
TN_DOWN = 2048
TM_CMB = 128
ROUTE_STRIDE = 2053


def _cparams(sem):
    return pltpu.CompilerParams(dimension_semantics=sem, vmem_limit_bytes=VMEM_LIMIT)


def _dot(a, b):
    return jnp.dot(a, b, preferred_element_type=F32)


def _dot_nt(a, b):
    return lax.dot_general(a, b, (((1,), (1,)), ((), ())), preferred_element_type=F32)


def _dot_tn(a, b):
    return lax.dot_general(a, b, (((0,), (0,)), ((), ())), preferred_element_type=F32)


def _split_bf16(x):
    hi = x.astype(BF16)
    lo = (x - hi.astype(F32)).astype(BF16)
    return hi, lo


def _dot3(a, b):
    ah, al = _split_bf16(a)
    bh, bl = _split_bf16(b)
    return _dot(ah, bh) + _dot(al, bh) + _dot(ah, bl)


def _ada_kernel(c_ref, w_ref, b_ref, o_ref, cab_ref):
    nb = c_ref.shape[0]
    d, tn = w_ref.shape
    rk = 64

    @pl.when(jnp.logical_and(pl.program_id(0) == 0, pl.program_id(1) == 0))
    def _():
        for b in range(nb):
            cc = c_ref[b]
            cab_ref[b] = jnp.broadcast_to(cc * jax.nn.sigmoid(cc), (d, LANES))

    def body(i, accs):
        r0 = pl.multiple_of(i * rk, rk)
        wk = w_ref[pl.ds(r0, rk), :]
        out = []
        for b in range(nb):
            ca = cab_ref[b, pl.ds(r0, rk), :]
            prod = jnp.concatenate([wk[:, q * LANES:(q + 1) * LANES] * ca for q in range(tn // LANES)], axis=1)
            out.append(accs[b] + jnp.sum(prod.reshape(rk // SUBLANES, SUBLANES, tn), axis=0))
        return tuple(out)

    accs = lax.fori_loop(0, d // rk, body, tuple(jnp.zeros((SUBLANES, tn), F32) for _ in range(nb)))
    for b in range(nb):
        o_ref[b:b + 1, :] = jnp.sum(accs[b], axis=0, keepdims=True) + b_ref[...]


def _ada(c, ada_w, ada_b):
    n_layers, d, n = ada_w.shape
    nb = c.shape[0]
    tn = 512
    return pl.pallas_call(
        _ada_kernel,
        out_shape=jax.ShapeDtypeStruct((n_layers, nb, n), F32),
        grid=(n_layers, n // tn),
        in_specs=[
            pl.BlockSpec((nb, d, 1), lambda l, j: (0, 0, 0)),
            pl.BlockSpec((None, d, tn), lambda l, j: (l, 0, j)),
            pl.BlockSpec((None, 1, tn), lambda l, j: (l, 0, j)),
        ],
        out_specs=pl.BlockSpec((None, nb, tn), lambda l, j: (l, 0, j)),
        scratch_shapes=[pltpu.VMEM((nb, d, LANES), F32)],
        compiler_params=_cparams(("arbitrary", "arbitrary")),
        name="ada_mod",
    )(c.reshape(nb, d, 1), ada_w, ada_b.reshape(n_layers, 1, n))


def _rms(x):
    return x * lax.rsqrt(jnp.mean(x * x, axis=-1, keepdims=True) + EPS)


def _prenorm_kernel(x_ref, g_ref, mod_ref, o_ref, *, shift_i, scale_i):
    m = mod_ref[...]
    h = _rms(x_ref[...]) * g_ref[...]
    o_ref[...] = (h * (1.0 + m[scale_i:scale_i + 1]) + m[shift_i:shift_i + 1]).astype(o_ref.dtype)


def _prenorm(x2, gain, mod6, seq, *, shift_i, scale_i):
    t, d = x2.shape
    per_b = seq // TM_ROW
    return pl.pallas_call(
        functools.partial(_prenorm_kernel, shift_i=shift_i, scale_i=scale_i),
        out_shape=jax.ShapeDtypeStruct((t, d), BF16),
        grid=(t // TM_ROW,),
        in_specs=[
            pl.BlockSpec((TM_ROW, d), lambda i: (i, 0)),
            pl.BlockSpec((1, d), lambda i: (0, 0)),
            pl.BlockSpec((None, N_MOD, d), lambda i: (i // per_b, 0, 0)),
        ],
        out_specs=pl.BlockSpec((TM_ROW, d), lambda i: (i, 0)),
        compiler_params=_cparams(("arbitrary",)),
        name="prenorm",
    )(x2, gain.reshape(1, d), mod6)


def _mm_kernel(x_ref, w_ref, o_ref, wbf_ref):
    @pl.when(pl.program_id(1) == 0)
    def _():
        wbf_ref[...] = w_ref[...].astype(BF16)

    o_ref[...] = _dot_nt(x_ref[...], wbf_ref[...]).astype(o_ref.dtype)


def _mm(x, w_t, layer, row_off, width, *, tn=TN_MM, out_dtype=F32):
    t, k = x.shape
    jb = row_off // tn
    return pl.pallas_call(
        _mm_kernel,
        out_shape=jax.ShapeDtypeStruct((t, width), out_dtype),
        grid=(width // tn, t // TM_MM),
        in_specs=[
            pl.BlockSpec((TM_MM, k), lambda j, i: (i, 0)),
            pl.BlockSpec((None, tn, k), lambda j, i: (layer, jb + j, 0)),
        ],
        out_specs=pl.BlockSpec((TM_MM, tn), lambda j, i: (i, j)),
        scratch_shapes=[pltpu.VMEM((tn, k), BF16)],
        compiler_params=_cparams(("arbitrary", "arbitrary")),
        name="proj_in",
    )(x, w_t)


def _mm_shift_kernel(x_ref, wa_ref, wb_ref, o_ref, wbf_ref, *, shift):
    @pl.when(pl.program_id(1) == 0)
    def _():
        tn = wa_ref.shape[0]
        wbf_ref[0:tn - shift, :] = wa_ref[shift:, :].astype(BF16)
        wbf_ref[tn - shift:, :] = wb_ref[...].astype(BF16)

    o_ref[...] = _dot_nt(x_ref[...], wbf_ref[...]).astype(o_ref.dtype)


def _mm_shift(x, w_t, layer, row_off, width, *, out_dtype=F32):
    t, k = x.shape
    tn = TN_MM
    jb, shift = divmod(row_off, tn)
    assert shift % 16 == 0 and tn % shift == 0
    return pl.pallas_call(
        functools.partial(_mm_shift_kernel, shift=shift),
        out_shape=jax.ShapeDtypeStruct((t, width), out_dtype),
        grid=(width // tn, t // TM_MM),
        in_specs=[
            pl.BlockSpec((TM_MM, k), lambda j, i: (i, 0)),
            pl.BlockSpec((None, tn, k), lambda j, i: (layer, jb + j, 0)),
            pl.BlockSpec((None, shift, k), lambda j, i: (layer, (jb + j + 1) * (tn // shift), 0)),
        ],
        out_specs=pl.BlockSpec((TM_MM, tn), lambda j, i: (i, j)),
        scratch_shapes=[pltpu.VMEM((tn, k), BF16)],
        compiler_params=_cparams(("arbitrary", "arbitrary")),
        name="proj_in_shift",
    )(x, w_t, w_t)


def _mm3_kernel(a_ref, b_ref, c_ref, w_ref, o_ref, wbf_ref):
    @pl.when(pl.program_id(1) == 0)
    def _():
        wbf_ref[...] = w_ref[...].astype(BF16)

    ka, kb = a_ref.shape[1], b_ref.shape[1]
    acc = _dot(a_ref[...], wbf_ref[0:ka, :])
    acc += _dot(b_ref[...], wbf_ref[ka:ka + kb, :])
    acc += _dot(c_ref[...], wbf_ref[ka + kb:, :])
    o_ref[...] = acc


def _mm_out(ya, yb, yc, w_full, layer):
    t = ya.shape[0]
    k, n = w_full.shape[1], w_full.shape[2]
    return pl.pallas_call(
        _mm3_kernel,
        out_shape=jax.ShapeDtypeStruct((t, n), F32),
        grid=(n // TN_MM, t // TM_MM),
        in_specs=[
            pl.BlockSpec((TM_MM, ya.shape[1]), lambda j, i: (i, 0)),
            pl.BlockSpec((TM_MM, yb.shape[1]), lambda j, i: (i, 0)),
            pl.BlockSpec((TM_MM, yc.shape[1]), lambda j, i: (i, 0)),
            pl.BlockSpec((None, k, TN_MM), lambda j, i: (layer, 0, j)),
        ],
        out_specs=pl.BlockSpec((TM_MM, TN_MM), lambda j, i: (i, j)),
        scratch_shapes=[pltpu.VMEM((k, TN_MM), BF16)],
        compiler_params=_cparams(("arbitrary", "arbitrary")),
        name="proj_out",
    )(ya, yb, yc, w_full)


def _ssm_params(lam_re, lam_im, log_step, b_re, b_im, c_re, c_im):
    step = jnp.exp(log_step)[:, None]
    ar, ai = lam_re * step, lam_im * step
    mag = jnp.exp(ar)
    lbr, lbi = mag * jnp.cos(ai), mag * jnp.sin(ai)
    den = lam_re * lam_re + lam_im * lam_im
    cfr = ((lbr - 1.0) * lam_re + lbi * lam_im) / den
    cfi = (lbi * lam_re - (lbr - 1.0) * lam_im) / den
    bbr = cfr[..., None] * b_re - cfi[..., None] * b_im
    bbi = cfr[..., None] * b_im + cfi[..., None] * b_re
    nj = SSM_GROUPS // SSM_GB
    eye = jnp.eye(SSM_GB, dtype=F32)

    def bmat(bb):
        bb = bb.reshape(nj, SSM_GB, SSM_STATE, SSM_GROUP)
        m = jnp.einsum("ab,jbph->jahbp", eye, bb)
        return m.reshape(nj, SSM_GB * SSM_GROUP, SSM_GB * SSM_STATE)

    def cmat(cc):
        cc = cc.reshape(nj, SSM_GB, SSM_GROUP, SSM_STATE)
        m = jnp.einsum("ab,jbhp->japbh", eye, cc)
        return m.reshape(nj, SSM_GB * SSM_STATE, SSM_GB * SSM_GROUP)

    b_mat = jnp.concatenate([bmat(bbr), bmat(bbi)], axis=-1).astype(BF16)
    c_mat = jnp.concatenate([cmat(c_re), -cmat(c_im)], axis=1).astype(BF16)

    def power(kk):
        m = jnp.exp(kk * ar[None])
        return ((m * jnp.cos(kk * ai[None])).reshape(-1, SSM_LANES),
                (m * jnp.sin(kk * ai[None])).reshape(-1, SSM_LANES))

    rows = jnp.arange(SUBLANES, dtype=F32)[:, None]
    shifts = []
    for s in (1, 2, 4):
        pr, pi = power(jnp.full((1, 1, 1), float(s), F32))
        keep = rows >= s
        shifts += [jnp.where(keep, pr, 0.0), jnp.where(keep, pi, 0.0)]
    pr, pi = power(jnp.arange(1, SUBLANES + 1, dtype=F32)[:, None, None])
    lam_tab = jnp.stack(shifts + [pr, pi])
    return b_mat, c_mat, lam_tab


def _ssm_kernel(u_ref, bm_ref, cm_ref, lt_ref, d_ref, gw_ref, gb_ref, o_ref, xr_ref, xi_ref, st_ref):
    tb = u_ref.shape[0]
    nj = bm_ref.shape[0]
    gw = SSM_GB * SSM_GROUP
    sw = SSM_GB * SSM_STATE

    @pl.when(pl.program_id(1) == 0)
    def _():
        st_ref[...] = jnp.zeros_like(st_ref)

    u = u_ref[...]
    ub = u.astype(BF16)
    for j in range(nj):
        bu = _dot(ub[:, j * gw:(j + 1) * gw], bm_ref[j])
        xr_ref[:, j * sw:(j + 1) * sw] = bu[:, :sw]
        xi_ref[:, j * sw:(j + 1) * sw] = bu[:, sw:]

    def row_body(r, carry):
        rs = pl.ds(pl.multiple_of(r * SUBLANES, SUBLANES), SUBLANES)
        for c in range(SSM_LANES // SSM_LC):
            sl = slice(c * SSM_LC, (c + 1) * SSM_LC)
            xr = xr_ref[rs, sl]
            xi = xi_ref[rs, sl]
            for k, s in enumerate((1, 2, 4)):
                lr, li = lt_ref[2 * k, :, sl], lt_ref[2 * k + 1, :, sl]
                sr = pltpu.roll(xr, s, axis=0)
                si = pltpu.roll(xi, s, axis=0)
                xr, xi = xr + (lr * sr - li * si), xi + (lr * si + li * sr)
            pr, pi = lt_ref[6, :, sl], lt_ref[7, :, sl]
            cr, ci = st_ref[0:1, sl], st_ref[1:2, sl]
            xr, xi = xr + (pr * cr - pi * ci), xi + (pr * ci + pi * cr)
            xr_ref[rs, sl] = xr
            xi_ref[rs, sl] = xi
            st_ref[0:1, sl] = xr[SUBLANES - 1:, :]
            st_ref[1:2, sl] = xi[SUBLANES - 1:, :]
        return carry

    lax.fori_loop(0, tb // SUBLANES, row_body, 0)

    ys = []
    for j in range(nj):
        xs = jnp.concatenate([xr_ref[:, j * sw:(j + 1) * sw], xi_ref[:, j * sw:(j + 1) * sw]], axis=-1)
        ys.append(_dot(xs.astype(BF16), cm_ref[j]))
    y = jnp.concatenate(ys, axis=-1) + d_ref[...] * u
    y = jax.nn.gelu(y)
    gate = _dot(y.astype(BF16), gw_ref[...]) + gb_ref[...]
    o_ref[...] = (y * jax.nn.sigmoid(gate)).astype(o_ref.dtype)


def _ssm(u, b_mat, c_mat, lam_tab, d_skip, glu_w, glu_b, nbatch):
    t, w = u.shape
    per_b = t // nbatch // TB_SSM
    full = lambda a: pl.BlockSpec(a.shape, lambda b, i: (0,) * a.ndim)
    d_row = d_skip.reshape(1, w)
    gb_row = glu_b.reshape(1, w)
    return pl.pallas_call(
        _ssm_kernel,
        out_shape=jax.ShapeDtypeStruct((t, w), BF16),
        grid=(nbatch, per_b),
        in_specs=[
            pl.BlockSpec((TB_SSM, w), lambda b, i: (b * per_b + i, 0)),
            full(b_mat), full(c_mat), full(lam_tab), full(d_row), full(glu_w), full(gb_row),
        ],
        out_specs=pl.BlockSpec((TB_SSM, w), lambda b, i: (b * per_b + i, 0)),
        scratch_shapes=[
            pltpu.VMEM((TB_SSM, SSM_LANES), F32),
            pltpu.VMEM((TB_SSM, SSM_LANES), F32),
            pltpu.VMEM((2, SSM_LANES), F32),
        ],
        compiler_params=_cparams(("arbitrary", "arbitrary")),
        name="s5_mixer",
    )(u, b_mat, c_mat, lam_tab, d_row, glu_w, gb_row)


def _log_sigmoid(z):
    return jnp.minimum(z, 0.0) - jnp.log1p(jnp.exp(-jnp.abs(z)))


def _silu(x):
    return x * jax.nn.sigmoid(x)


def _gla_kernel(qk_ref, v_ref, g_ref, a_ref, wa_ref, ba_ref, gain_ref, o_ref, st_ref):
    tb = qk_ref.shape[0]
    ck = GLA_CHUNK

    @pl.when(pl.program_id(1) == 0)
    def _():
        st_ref[...] = jnp.zeros_like(st_ref)

    log_a = _log_sigmoid(_dot3(a_ref[:, :GLA_RANK], wa_ref[...]) + ba_ref[...]) * (1.0 / GLA_TAU)
    ri = lax.broadcasted_iota(I32, (ck, ck), 0)
    ci = lax.broadcasted_iota(I32, (ck, ck), 1)
    causal = ri >= ci
    tri = jnp.where(causal, 1.0, 0.0).astype(BF16)
    for c in range(tb // ck):
        rows = slice(c * ck, (c + 1) * ck)
        la_hi, la_lo = _split_bf16(log_a[rows])
        dec = _dot(tri, la_hi) + _dot(tri, la_lo)
        for h in range(GLA_HEADS):
            kcol = slice(h * GLA_DK, (h + 1) * GLA_DK)
            vcol = slice(h * GLA_DV, (h + 1) * GLA_DV)
            b = dec[:, kcol]
            b_last = b[ck - 1:ck, :]
            q = qk_ref[rows, kcol] * (GLA_DK ** -0.5)
            k = qk_ref[rows, GLA_QK + h * GLA_DK:GLA_QK + (h + 1) * GLA_DK]
            v = v_ref[rows, vcol].astype(BF16)
            q_dec = (q * jnp.exp(b)).astype(BF16)
            k_intra = (k * jnp.exp(-b)).astype(BF16)
            k_state = (k * jnp.exp(b_last - b)).astype(BF16)
            scores = jnp.where(causal, _dot_nt(q_dec, k_intra), 0.0)
            st = st_ref[h]
            o = _dot(scores.astype(BF16), v) + _dot_nt(q_dec, st.astype(BF16))
            st_ref[h] = st * jnp.exp(b_last) + _dot_tn(v, k_state)
            o = _rms(o) * gain_ref[:, vcol] * _silu(g_ref[rows, vcol])
            o_ref[rows, vcol] = o.astype(o_ref.dtype)


def _gla(gqk, gv, gg, ga, wa, ba, gain, nbatch):
    t = gqk.shape[0]
    per_b = t // nbatch // TB_GLA
    row = lambda w: pl.BlockSpec((TB_GLA, w), lambda b, i: (b * per_b + i, 0))
    full = lambda a: pl.BlockSpec(a.shape, lambda b, i: (0,) * a.ndim)
    ba_row = ba.reshape(1, GLA_QK)
    gain_row = gain.reshape(1, GLA_WIDTH)
    return pl.pallas_call(
        _gla_kernel,
        out_shape=jax.ShapeDtypeStruct((t, GLA_WIDTH), BF16),
        grid=(nbatch, per_b),
        in_specs=[row(2 * GLA_QK), row(GLA_WIDTH), row(GLA_WIDTH), row(LANES),
                  full(wa), full(ba_row), full(gain_row)],
        out_specs=row(GLA_WIDTH),
        scratch_shapes=[pltpu.VMEM((GLA_HEADS, GLA_DV, GLA_DK), F32)],
        compiler_params=_cparams(("arbitrary", "arbitrary")),
        name="gla_mixer",
    )(gqk, gv, gg, ga, wa, ba_row, gain_row)


def _ret_kernel(qk_ref, v_ref, g_ref, pos_ref, inv_ref, gain_ref, o_ref, st_ref):
    tb = qk_ref.shape[0]

    @pl.when(pl.program_id(1) == 0)
    def _():
        st_ref[...] = jnp.zeros_like(st_ref)

    ang = pos_ref[...].astype(F32) * inv_ref[...]
    lane = lax.broadcasted_iota(I32, (1, RET_DK), 1)
    cos = jnp.cos(ang)
    sin = jnp.sin(ang) * jnp.where(lane < RET_DK // 2, -1.0, 1.0)
    ri = lax.broadcasted_iota(I32, (tb, tb), 0)
    ci = lax.broadcasted_iota(I32, (tb, tb), 1)
    rel = (ri - ci).astype(F32)
    idx = lax.broadcasted_iota(I32, (tb, 1), 0).astype(F32)
    for h in range(RET_HEADS):
        log_gamma = math.log1p(-(2.0 ** (-5.0 - h)))
        kcol = slice(h * RET_DK, (h + 1) * RET_DK)
        vcol = slice(h * RET_DV, (h + 1) * RET_DV)
        q = qk_ref[:, kcol]
        k = qk_ref[:, RET_QK + h * RET_DK:RET_QK + (h + 1) * RET_DK]
        q = (q * cos + pltpu.roll(q, RET_DK // 2, axis=1) * sin) * (RET_DK ** -0.5)
        k = k * cos + pltpu.roll(k, RET_DK // 2, axis=1) * sin
        v = v_ref[:, vcol].astype(BF16)
        decay = jnp.where(rel >= 0, jnp.exp(jnp.maximum(rel, 0.0) * log_gamma), 0.0)
        scores = _dot_nt(q.astype(BF16), k.astype(BF16)) * decay
        q_w = jnp.exp((idx + 1.0) * log_gamma)
        k_w = jnp.exp((tb - 1.0 - idx) * log_gamma)
        st = st_ref[h]
        o = _dot(scores.astype(BF16), v) + _dot((q * q_w).astype(BF16), st.astype(BF16))
        st_ref[h] = st * math.exp(tb * log_gamma) + _dot_tn((k * k_w).astype(BF16), v)
        o = _rms(o) * gain_ref[:, vcol] * _silu(g_ref[:, vcol])
        o_ref[:, vcol] = o.astype(o_ref.dtype)


def _ret(rqk, rv, rg, pos, inv, gain, nbatch):
    t = rqk.shape[0]
    per_b = t // nbatch // TB_RET
    row = lambda w: pl.BlockSpec((TB_RET, w), lambda b, i: (b * per_b + i, 0))
    full = lambda a: pl.BlockSpec(a.shape, lambda b, i: (0,) * a.ndim)
    gain_row = gain.reshape(1, RET_WIDTH)
    return pl.pallas_call(
        _ret_kernel,
        out_shape=jax.ShapeDtypeStruct((t, RET_WIDTH), BF16),
        grid=(nbatch, per_b),
        in_specs=[row(2 * RET_QK), row(RET_WIDTH), row(RET_WIDTH), row(1), full(inv), full(gain_row)],
        out_specs=row(RET_WIDTH),
        scratch_shapes=[pltpu.VMEM((RET_HEADS, RET_DK, RET_DV), F32)],
        compiler_params=_cparams(("arbitrary", "arbitrary")),
        name="ret_mixer",
    )(rqk, rv, rg, pos, inv, gain_row)


def _mix_post_kernel(y_ref, x_ref, mod_ref, gpost_ref, gpre_ref, rwh_ref, rwl_ref, rb_ref,
                     xo_ref, ho_ref, ti_ref, tw_ref):
    m = mod_ref[...]
    x = x_ref[...] + m[2:3] * (_rms(y_ref[...]) * gpost_ref[...])
    xo_ref[...] = x
    h = _rms(x) * gpre_ref[...] * (1.0 + m[4:5]) + m[3:4]
    ho_ref[...] = h
    hh, hl = _split_bf16(h)
    logits = _dot(hh, rwh_ref[...]) + _dot(hl, rwh_ref[...]) + _dot(hh, rwl_ref[...]) + rb_ref[...]
    lane = lax.broadcasted_iota(I32, logits.shape, 1)
    vals, idxs = [], []
    for _ in range(TOP_K):
        mx = jnp.max(logits, axis=-1, keepdims=True)
        ix = jnp.min(jnp.where(logits == mx, lane, LANES), axis=-1, keepdims=True)
        vals.append(mx)
        idxs.append(ix)
        logits = jnp.where(lane == ix, -jnp.inf, logits)
    es = [jnp.exp(v - vals[0]) for v in vals]
    inv_sum = 1.0 / (es[0] + es[1] + es[2] + es[3])
    ti = jnp.zeros(lane.shape, I32)
    tw = jnp.zeros(lane.shape, F32)
    for kk in range(TOP_K):
        ti = jnp.where(lane == kk, idxs[kk], ti)
        tw = jnp.where(lane == kk, es[kk] * inv_sum, tw)
    ti_ref[...] = ti
    tw_ref[...] = tw


def _mix_post(y, x2, mod6, gpost, gpre, rw_hi, rw_lo, rb, seq):
    t, d = x2.shape
    per_b = seq // TM_ROW
    row = lambda w: pl.BlockSpec((TM_ROW, w), lambda i: (i, 0))
    full = lambda a: pl.BlockSpec(a.shape, lambda i: (0,) * a.ndim)
    gpost, gpre = gpost.reshape(1, d), gpre.reshape(1, d)
    return pl.pallas_call(
        _mix_post_kernel,
        out_shape=(jax.ShapeDtypeStruct((t, d), F32), jax.ShapeDtypeStruct((t, d), F32),
                   jax.ShapeDtypeStruct((t, LANES), I32), jax.ShapeDtypeStruct((t, LANES), F32)),
        grid=(t // TM_ROW,),
        in_specs=[row(d), row(d), pl.BlockSpec((None, N_MOD, d), lambda i: (i // per_b, 0, 0)),
                  full(gpost), full(gpre), full(rw_hi), full(rw_lo), full(rb)],
        out_specs=(row(d), row(d), row(LANES), row(LANES)),
        compiler_params=_cparams(("arbitrary",)),
        name="mix_post_router",
    )(y, x2, mod6, gpost, gpre, rw_hi, rw_lo, rb)


def _route(top_idx, n_tiles):
    t = top_idx.shape[0]
    assert math.gcd(ROUTE_STRIDE, t) == 1
    perm = (jnp.arange(t, dtype=I32) * ROUTE_STRIDE) % t
    inv_perm = (jnp.arange(t, dtype=I32) * pow(ROUTE_STRIDE, -1, t)) % t
    e_flat = top_idx[perm].reshape(-1)
    onehot =e_flat[:, None] == jnp.arange(N_EXPERTS, dtype=I32)[None, :]
    blk = 256
    nblk = (t * TOP_K) // blk
    oh3 = onehot.astype(BF16).reshape(nblk, blk, N_EXPERTS)
    within = jnp.einsum("ij,bjk->bik", jnp.tril(jnp.ones((blk, blk), BF16)), oh3, preferred_element_type=F32)
    tot = within[:, -1, :]
    before = jnp.dot(jnp.tril(jnp.ones((nblk, nblk), BF16), -1), tot.astype(BF16), preferred_element_type=F32)
    csum = (within + before[:, None, :]).reshape(t * TOP_K, N_EXPERTS)
    counts = (before[-1] + tot[-1]).astype(I32)
    padded = ((counts + TM_MOE - 1) // TM_MOE) * TM_MOE
    ends = jnp.cumsum(padded)
    starts = ends - padded
    dest = jnp.sum(jnp.where(onehot, csum - 1.0 + starts.astype(F32)[None, :], 0.0), axis=1).astype(I32)
    row_src = (jnp.arange(n_tiles * TM_MOE, dtype=I32) % t).at[dest].set(jnp.repeat(perm, TOP_K))
    n_used = ends[-1] // TM_MOE
    tile_ids = jnp.minimum(jnp.arange(n_tiles, dtype=I32), n_used - 1)
    tile_expert = jnp.sum(ends[None, :] <= (tile_ids * TM_MOE)[:, None], axis=1).astype(I32)
    pos_t = dest.reshape(t, TOP_K)[inv_perm].T.reshape(-1)
    return row_src, tile_expert, n_used.reshape(1).astype(I32), pos_t


def _new_weights(te_ref, m):
    prev = te_ref[jnp.maximum(m - 1, 0)]
    return jnp.logical_or(m == 0, te_ref[m] != prev)


def _gu_kernel(te_ref, nu_ref, src_ref, h_ref, wg_ref, wu_ref, bg_ref, bu_ref, o_ref,
               wcat_ref, xa_ref, xb_ref, sem, *, unrolled_issue):
    m = pl.program_id(1)
    nt = pl.num_programs(1)
    step = pl.program_id(0) * nt + m
    n_steps = pl.num_programs(0) * nt
    next_tile = jnp.where(m + 1 < nt, m + 1, 0)
    bufs = (xa_ref, xb_ref)

    def row_copy(src_row, slot, r):
        return pltpu.make_async_copy(h_ref.at[pl.ds(src_row, 1)], bufs[slot].at[pl.ds(r, 1)], sem.at[slot])

    def issue(tile, slot, unrolled):
        base = tile * TM_MOE
        if unrolled:
            for r in range(TM_MOE):
                row_copy(src_ref[base + r], slot, r).start()
        else:
            def body(r, carry):
                row_copy(src_ref[base + r], slot, r).start()
                return carry
            lax.fori_loop(0, TM_MOE, body, 0, unroll=8)

    def wait(slot):
        pltpu.make_async_copy(h_ref.at[pl.ds(0, TM_MOE)], bufs[slot], sem.at[slot]).wait()

    @pl.when(step == 0)
    def _():
        issue(0, 0, False)

    @pl.when(_new_weights(te_ref, m))
    def _():
        wcat_ref[:, 0:TN_GU] = wg_ref[...].astype(BF16)
        wcat_ref[:, TN_GU:] = wu_ref[...].astype(BF16)

    valid = m < nu_ref[0]
    for slot in range(2):
        mine = step % 2 == slot

        @pl.when(jnp.logical_and(mine, valid))
        def _(slot=slot):
            wait(slot)
            issue(next_tile, 1 - slot, unrolled_issue)
            x = bufs[slot][...].astype(BF16)
            gu = _dot(x, wcat_ref[...])
            gate = gu[:, 0:TN_GU] + bg_ref[...]
            up = gu[:, TN_GU:] + bu_ref[...]
            gate = jnp.minimum(gate, SWIGLU_LIMIT)
            up = jnp.clip(up, -SWIGLU_LIMIT, SWIGLU_LIMIT)
            o_ref[...] = ((up + 1.0) * gate * jax.nn.sigmoid(SWIGLU_ALPHA * gate)).astype(o_ref.dtype)

        @pl.when(jnp.logical_and(mine, jnp.logical_not(valid)))
        def _(slot=slot):
            wait(slot)
            issue(next_tile, 1 - slot, False)
            o_ref[...] = jnp.zeros_like(o_ref)

    for slot in range(2):
        @pl.when(jnp.logical_and(step == n_steps - 1, step % 2 == 1 - slot))
        def _(slot=slot):
            wait(slot)


def _expert_gu(tile_expert, n_used, row_src, h, w_gu, b_gu4, layer, *, unrolled_issue=True):
    mrows = row_src.shape[0]
    nt = mrows // TM_MOE
    nn = D_EXPERT // TN_GU
    wspec = lambda off: pl.BlockSpec((None, None, D_MODEL, TN_GU),
                                     lambda n, m, te, nu, src: (layer, te[m], 0, off + n))
    bspec = lambda off: pl.BlockSpec((None, None, 1, TN_GU), lambda n, m, te, nu, src: (layer, te[m], 0, off + n))
    return pl.pallas_call(
        functools.partial(_gu_kernel, unrolled_issue=unrolled_issue),
        out_shape=jax.ShapeDtypeStruct((mrows, D_EXPERT), BF16),
        grid_spec=pltpu.PrefetchScalarGridSpec(
            num_scalar_prefetch=3,
            grid=(nn, nt),
            in_specs=[pl.BlockSpec(memory_space=pl.ANY), wspec(0), wspec(nn), bspec(0), bspec(nn)],
            out_specs=pl.BlockSpec((TM_MOE, TN_GU), lambda n, m, te, nu, src: (m, n)),
            scratch_shapes=[pltpu.VMEM((D_MODEL, 2 * TN_GU), BF16),
                            pltpu.VMEM((TM_MOE, D_MODEL), F32), pltpu.VMEM((TM_MOE, D_MODEL), F32),
                            pltpu.SemaphoreType.DMA((2,))],
        ),
        compiler_params=_cparams(("arbitrary", "arbitrary")),
        name="moe_gate_up",
    )(tile_expert, n_used, row_src, h, w_gu, w_gu, b_gu4, b_gu4)


def _down_kernel(te_ref, nu_ref, a_ref, w_ref, b_ref, o_ref, wb_ref):
    m = pl.program_id(1)

    @pl.when(_new_weights(te_ref, m))
    def _():
        wb_ref[...] = w_ref[...].astype(BF16)

    @pl.when(m < nu_ref[0])
    def _():
        o_ref[...] = _dot(a_ref[...], wb_ref[...]) + b_ref[...]

    @pl.when(m >= nu_ref[0])
    def _():
        o_ref[...] = jnp.zeros_like(o_ref)


def _expert_down(tile_expert, n_used, act, w_down, b_down4, layer):
    mrows = act.shape[0]
    nt = mrows // TM_MOE
    return pl.pallas_call(
        _down_kernel,
        out_shape=jax.ShapeDtypeStruct((mrows, D_MODEL), F32),
        grid_spec=pltpu.PrefetchScalarGridSpec(
            num_scalar_prefetch=2,
            grid=(D_MODEL // TN_DOWN, nt),
            in_specs=[
                pl.BlockSpec((TM_MOE, D_EXPERT), lambda n, m, te, nu: (jnp.minimum(m, nu[0] - 1), 0)),
                pl.BlockSpec((None, None, D_EXPERT, TN_DOWN), lambda n, m, te, nu: (layer, te[m], 0, n)),
                pl.BlockSpec((None, None, 1, TN_DOWN), lambda n, m, te, nu: (layer, te[m], 0, n)),
            ],
            out_specs=pl.BlockSpec((TM_MOE, TN_DOWN), lambda n, m, te, nu: (m, n)),
            scratch_shapes=[pltpu.VMEM((D_EXPERT, TN_DOWN), BF16)],
        ),
        compiler_params=_cparams(("arbitrary", "arbitrary")),
        name="moe_down",
    )(tile_expert, n_used, act, w_down, b_down4)


def _combine_kernel(pos_ref, ys_ref, tw_ref, x_ref, mod_ref, gain_ref, o_ref, buf_ref, sem):
    i = pl.program_id(0)
    n = pl.num_programs(0)
    tm = x_ref.shape[0]
    t_total = n * tm
    rows = TOP_K * tm

    def issue(tile, slot, unrolled):
        for kk in range(TOP_K):
            base = kk * t_total + tile * tm

            def start(r, kk=kk, base=base):
                pltpu.make_async_copy(ys_ref.at[pl.ds(pos_ref[base + r], 1)],
                                      buf_ref.at[slot, pl.ds(kk * tm + r, 1)], sem.at[slot]).start()

            if unrolled:
                for r in range(tm):
                    start(r)
            else:
                def body(r, carry, start=start):
                    start(r)
                    return carry
                lax.fori_loop(0, tm, body, 0, unroll=8)

    @pl.when(i == 0)
    def _():
        issue(0, 0, False)

    for slot in range(2):
        @pl.when(i % 2 == slot)
        def _(slot=slot):
            @pl.when(i + 1 < n)
            def _():
                issue(i + 1, 1 - slot, True)

            pltpu.make_async_copy(ys_ref.at[pl.ds(0, rows)], buf_ref.at[slot], sem.at[slot]).wait()
            tw = tw_ref[...]
            y = tw[:, 0:1] * buf_ref[slot, 0:tm, :]
            for kk in range(1, TOP_K):
                y += tw[:, kk:kk + 1] * buf_ref[slot, kk * tm:(kk + 1) * tm, :]
            m = mod_ref[...]
            o_ref[...] = x_ref[...] + m[5:6] * (_rms(y) * gain_ref[...])


def _combine(pos_t, ys, top_w, x2, mod6, gain, seq):
    t, d = x2.shape
    per_b = seq // TM_CMB
    gain = gain.reshape(1, d)
    return pl.pallas_call(
        _combine_kernel,
        out_shape=jax.ShapeDtypeStruct((t, d), F32),
        grid_spec=pltpu.PrefetchScalarGridSpec(
            num_scalar_prefetch=1,
            grid=(t // TM_CMB,),
            in_specs=[
                pl.BlockSpec(memory_space=pl.ANY),
                pl.BlockSpec((TM_CMB, LANES), lambda i, pos: (i, 0)),
                pl.BlockSpec((TM_CMB, d), lambda i, pos: (i, 0)),
                pl.BlockSpec((None, N_MOD, d), lambda i, pos: (i // per_b, 0, 0)),
                pl.BlockSpec((1, d), lambda i, pos: (0, 0)),
            ],
            out_specs=pl.BlockSpec((TM_CMB, d), lambda i, pos: (i, 0)),
            scratch_shapes=[pltpu.VMEM((2, TOP_K * TM_CMB, d), F32), pltpu.SemaphoreType.DMA((2,))],
        ),
        compiler_params=_cparams(("arbitrary",)),
        name="moe_combine",
    )(pos_t, ys, top_w, x2, mod6, gain)


def kernel(x, c, positions, ada_w, ada_b, mix_pre_gain, mix_post_gain, ffn_pre_gain, ffn_post_gain,
           w_in, w_out, ssm_lam_re, ssm_lam_im, ssm_log_step, ssm_b_re, ssm_b_im, ssm_c_re, ssm_c_im,
           ssm_d, ssm_glu_w, ssm_glu_b, gla_wa, gla_ba, gla_norm_gain, ret_norm_gain,
           router_w, router_b, exp_w_gu, exp_b_gu, exp_w_down, exp_b_down):
    nbatch, seq, d = x.shape
    t = nbatch * seq
    depth = ada_w.shape[0]
    n_tiles = (t * TOP_K) // TM_MOE + N_EXPERTS

    x2 = x.reshape(t, d)
    pos = positions.reshape(t, 1)
    half = RET_DK // 2
    inv = jnp.power(ROPE_BASE, -jnp.arange(half, dtype=F32) / half)
    inv = jnp.concatenate([inv, inv]).reshape(1, RET_DK)
    mod = _ada(c, ada_w, ada_b)

    rw =jnp.pad(router_w, ((0, 0), (0, 0), (0, LANES - N_EXPERTS)))
    rw_hi = rw.astype(BF16)
    rw_lo = (rw - rw_hi.astype(F32)).astype(BF16)
    rb = jnp.pad(router_b, ((0, 0), (0, LANES - N_EXPERTS)), constant_values=-1e30)
    b_gu4 = exp_b_gu.reshape(depth, N_EXPERTS, 1, 2 * D_EXPERT)
    b_down4 = exp_b_down.reshape(depth, N_EXPERTS, 1, d)
    glu_w = ssm_glu_w.astype(BF16)
    w_in_t = jnp.swapaxes(w_in, 1, 2)

    for l in range(depth):
        mod6 = mod[l].reshape(nbatch, N_MOD, d)
        h = _prenorm(x2, mix_pre_gain[l], mod6, seq, shift_i=0, scale_i=1)
        u = _mm(h, w_in_t, l, OFF_U, SSM_WIDTH)
        gqk = _mm(h, w_in_t, l, OFF_GQK, 2 * GLA_QK)
        gv = _mm(h, w_in_t, l, OFF_GV, GLA_WIDTH)
        gg = _mm(h, w_in_t, l, OFF_GG, GLA_WIDTH)
        ga = _mm(h, w_in_t, l, OFF_GA, LANES, tn=LANES)
        rqk = _mm_shift(h, w_in_t, l, OFF_TAIL, 2 * RET_QK)
        rv = _mm_shift(h, w_in_t, l, OFF_TAIL + 2 * RET_QK, RET_WIDTH)
        rg = _mm_shift(h, w_in_t, l, OFF_TAIL + 2 * RET_QK + RET_WIDTH, RET_WIDTH)

        b_mat, c_mat, lam_tab = _ssm_params(ssm_lam_re[l], ssm_lam_im[l], ssm_log_step[l], ssm_b_re[l],
                                            ssm_b_im[l], ssm_c_re[l], ssm_c_im[l])
        y_ssm = _ssm(u, b_mat, c_mat, lam_tab, ssm_d[l], glu_w[l], ssm_glu_b[l], nbatch)
        y_gla = _gla(gqk, gv, gg, ga, gla_wa[l], gla_ba[l], gla_norm_gain[l], nbatch)
        y_ret = _ret(rqk, rv, rg, pos, inv, ret_norm_gain[l], nbatch)
        y = _mm_out(y_ssm, y_gla, y_ret, w_out, l)

        x2, h2, top_idx, top_w = _mix_post(y, x2, mod6, mix_post_gain[l], ffn_pre_gain[l],
                                           rw_hi[l], rw_lo[l], rb[l:l + 1], seq)
        row_src, tile_expert, n_used, pos_t = _route(top_idx[:, :TOP_K], n_tiles)
        act = _expert_gu(tile_expert, n_used, row_src, h2, exp_w_gu, b_gu4, l)
        ys = _expert_down(tile_expert, n_used, act, exp_w_down, b_down4, l)
        x2 = _combine(pos_t, ys, top_w, x2, mod6, ffn_post_gain[l], seq)
    return x2.reshape(nbatch, seq, d)
```

```python
import functools
import math

import jax
import jax.numpy as jnp
from jax import lax
from jax.experimental import pallas as pl
from jax.experimental.pallas import tpu as pltpu

F32 = jnp.float32
BF16 = jnp.bfloat16
I32 = jnp.int32

D_MODEL = 4096
N_MOD = 6
SSM_WIDTH = 1024
SSM_GROUP = 16
SSM_GROUPS = 64
SSM_STATE = 64
SSM_LANES = SSM_GROUPS * SSM_STATE
GLA_WIDTH = 1536
GLA_HEADS = 4
GLA_DV = 384
GLA_DK = 192
GLA_QK = 768
GLA_RANK = 16
GLA_TAU = 16.0
GLA_CHUNK = 64
RET_WIDTH = 1536
RET_HEADS = 6
RET_DV = 256
RET_DK = 128
RET_QK = 768
ROPE_BASE = 10000.0
N_EXPERTS = 32
TOP_K = 4
D_EXPERT = 768
SWIGLU_LIMIT = 7.0
SWIGLU_ALPHA = 1.702
EPS = 1e-6
OFF_U = 0
OFF_GQK = 1024
OFF_GV = 2560
OFF_GG = 4096
OFF_GA = 5632
OFF_TAIL = 5648

LANES = 128
SUBLANES = 8
VMEM_LIMIT = 52 * 1024 * 1024

TM_ROW = 256
TM_MM = 1024
TN_MM = 512
TB_SSM = 256
SSM_LC = 512
SSM_GB = 16
TB_GLA = 256
TB_RET = 256
TM_MOE = 256
TN_GU = 384
TN_DOWN = 4096
TM_CMB = 128
ROUTE_STRIDE = 2053


def _cparams(sem):
    return pltpu.CompilerParams(dimension_semantics=sem, vmem_limit_bytes=VMEM_LIMIT)


def _dot(a, b):
    return jnp.dot(a, b, preferred_element_type=F32)


def _dot_nt(a, b):
    return lax.dot_general(a, b, (((1,), (1,)), ((), ())), preferred_element_type=F32)


def _dot_tn(a, b):
    return lax.dot_general(a, b, (((0,), (0,)), ((), ())), preferred_element_type=F32)


def _split_bf16(x):
    hi = x.astype(BF16)
    lo = (x - hi.astype(F32)).astype(BF16)
    return hi, lo


def _dot3(a, b):
    ah, al = _split_bf16(a)
    bh, bl = _split_bf16(b)
    return _dot(ah, bh) + _dot(al, bh) + _dot(ah, bl)


def _ada_kernel(c_ref, w_ref, b_ref, o_ref, cab_ref):
    nb = c_ref.shape[0]
    d, tn = w_ref.shape
    rk = 64

    @pl.when(jnp.logical_and(pl.program_id(0) == 0, pl.program_id(1) == 0))
    def _():
        for b in range(nb):
            cc = c_ref[b]
            cab_ref[b] = jnp.broadcast_to(cc * jax.nn.sigmoid(cc), (d, LANES))

    def body(i, accs):
        r0 = pl.multiple_of(i * rk, rk)
        wk = w_ref[pl.ds(r0, rk), :]
        out = []
        for b in range(nb):
            ca = cab_ref[b, pl.ds(r0, rk), :]
            prod = jnp.concatenate([wk[:, q * LANES:(q + 1) * LANES] * ca for q in range(tn // LANES)], axis=1)
            out.append(accs[b] + jnp.sum(prod.reshape(rk // SUBLANES, SUBLANES, tn), axis=0))
        return tuple(out)

    accs = lax.fori_loop(0, d // rk, body, tuple(jnp.zeros((SUBLANES, tn), F32) for _ in range(nb)))
    for b in range(nb):
        o_ref[b:b + 1, :] = jnp.sum(accs[b], axis=0, keepdims=True) + b_ref[...]


def _ada(c, ada_w, ada_b):
    n_layers, d, n = ada_w.shape
    nb = c.shape[0]
    tn = 512
    return pl.pallas_call(
        _ada_kernel,
        out_shape=jax.ShapeDtypeStruct((n_layers, nb, n), F32),
        grid=(n_layers, n // tn),
        in_specs=[
            pl.BlockSpec((nb, d, 1), lambda l, j: (0, 0, 0)),
            pl.BlockSpec((None, d, tn), lambda l, j: (l, 0, j)),
            pl.BlockSpec((None, 1, tn), lambda l, j: (l, 0, j)),
        ],
        out_specs=pl.BlockSpec((None, nb, tn), lambda l, j: (l, 0, j)),
        scratch_shapes=[pltpu.VMEM((nb, d, LANES), F32)],
        compiler_params=_cparams(("arbitrary", "arbitrary")),
        name="ada_mod",
    )(c.reshape(nb, d, 1), ada_w, ada_b.reshape(n_layers, 1, n))


def _rms(x):
    return x * lax.rsqrt(jnp.mean(x * x, axis=-1, keepdims=True) + EPS)


def _prenorm_kernel(x_ref, g_ref, mod_ref, o_ref, *, shift_i, scale_i):
    m = mod_ref[...]
    h = _rms(x_ref[...]) * g_ref[...]
    o_ref[...] = (h * (1.0 + m[scale_i:scale_i + 1]) + m[shift_i:shift_i + 1]).astype(o_ref.dtype)


def _prenorm(x2, gain, mod6, seq, *, shift_i, scale_i):
    t, d = x2.shape
    per_b = seq // TM_ROW
    return pl.pallas_call(
        functools.partial(_prenorm_kernel, shift_i=shift_i, scale_i=scale_i),
        out_shape=jax.ShapeDtypeStruct((t, d), BF16),
        grid=(t // TM_ROW,),
        in_specs=[
            pl.BlockSpec((TM_ROW, d), lambda i: (i, 0)),
            pl.BlockSpec((1, d), lambda i: (0, 0)),
            pl.BlockSpec((None, N_MOD, d), lambda i: (i // per_b, 0, 0)),
        ],
        out_specs=pl.BlockSpec((TM_ROW, d), lambda i: (i, 0)),
        compiler_params=_cparams(("arbitrary",)),
        name="prenorm",
    )(x2, gain.reshape(1, d), mod6)


def _mm_kernel(x_ref, w_ref, o_ref, wbf_ref):
    @pl.when(pl.program_id(1) == 0)
    def _():
        wbf_ref[...] = w_ref[...].astype(BF16)

    o_ref[...] = _dot_nt(x_ref[...], wbf_ref[...]).astype(o_ref.dtype)


def _mm(x, w_t, layer, row_off, width, *, tn=TN_MM, out_dtype=F32):
    t, k = x.shape
    jb = row_off // tn
    return pl.pallas_call(
        _mm_kernel,
        out_shape=jax.ShapeDtypeStruct((t, width), out_dtype),
        grid=(width // tn, t // TM_MM),
        in_specs=[
            pl.BlockSpec((TM_MM, k), lambda j, i: (i, 0)),
            pl.BlockSpec((None, tn, k), lambda j, i: (layer, jb + j, 0)),
        ],
        out_specs=pl.BlockSpec((TM_MM, tn), lambda j, i: (i, j)),
        scratch_shapes=[pltpu.VMEM((tn, k), BF16)],
        compiler_params=_cparams(("arbitrary", "arbitrary")),
        name="proj_in",
    )(x, w_t)


def _mm_shift_kernel(x_ref, wa_ref, wb_ref, o_ref, wbf_ref, *, shift):
    @pl.when(pl.program_id(1) == 0)
    def _():
        tn = wa_ref.shape[0]
        wbf_ref[0:tn - shift, :] = wa_ref[shift:, :].astype(BF16)
        wbf_ref[tn - shift:, :] = wb_ref[...].astype(BF16)

    o_ref[...] = _dot_nt(x_ref[...], wbf_ref[...]).astype(o_ref.dtype)


def _mm_shift(x, w_t, layer, row_off, width, *, out_dtype=F32):
    t, k = x.shape
    tn = TN_MM
    jb, shift = divmod(row_off, tn)
    assert shift % 16 == 0 and tn % shift == 0
    return pl.pallas_call(
        functools.partial(_mm_shift_kernel, shift=shift),
        out_shape=jax.ShapeDtypeStruct((t, width), out_dtype),
        grid=(width // tn, t // TM_MM),
        in_specs=[
            pl.BlockSpec((TM_MM, k), lambda j, i: (i, 0)),
            pl.BlockSpec((None, tn, k), lambda j, i: (layer, jb + j, 0)),
            pl.BlockSpec((None, shift, k), lambda j, i: (layer, (jb + j + 1) * (tn // shift), 0)),
        ],
        out_specs=pl.BlockSpec((TM_MM, tn), lambda j, i: (i, j)),
        scratch_shapes=[pltpu.VMEM((tn, k), BF16)],
        compiler_params=_cparams(("arbitrary", "arbitrary")),
        name="proj_in_shift",
    )(x, w_t, w_t)


def _mm3_kernel(a_ref, b_ref, c_ref, w_ref, o_ref, wbf_ref):
    @pl.when(pl.program_id(1) == 0)
    def _():
        wbf_ref[...] = w_ref[...].astype(BF16)

    ka, kb = a_ref.shape[1], b_ref.shape[1]
    acc = _dot(a_ref[...], wbf_ref[0:ka, :])
    acc += _dot(b_ref[...], wbf_ref[ka:ka + kb, :])
    acc += _dot(c_ref[...], wbf_ref[ka + kb:, :])
    o_ref[...] = acc


def _mm_out(ya, yb, yc, w_full, layer):
    t = ya.shape[0]
    k, n = w_full.shape[1], w_full.shape[2]
    return pl.pallas_call(
        _mm3_kernel,
        out_shape=jax.ShapeDtypeStruct((t, n), F32),
        grid=(n // TN_MM, t // TM_MM),
        in_specs=[
            pl.BlockSpec((TM_MM, ya.shape[1]), lambda j, i: (i, 0)),
            pl.BlockSpec((TM_MM, yb.shape[1]), lambda j, i: (i, 0)),
            pl.BlockSpec((TM_MM, yc.shape[1]), lambda j, i: (i, 0)),
            pl.BlockSpec((None, k, TN_MM), lambda j, i: (layer, 0, j)),
        ],
        out_specs=pl.BlockSpec((TM_MM, TN_MM), lambda j, i: (i, j)),
        scratch_shapes=[pltpu.VMEM((k, TN_MM), BF16)],
        compiler_params=_cparams(("arbitrary", "arbitrary")),
        name="proj_out",
    )(ya, yb, yc, w_full)


def _ssm_params(lam_re, lam_im, log_step, b_re, b_im, c_re, c_im):
    step = jnp.exp(log_step)[:, None]
    ar, ai = lam_re * step, lam_im * step
    mag = jnp.exp(ar)
    lbr, lbi = mag * jnp.cos(ai), mag * jnp.sin(ai)
    den = lam_re * lam_re + lam_im * lam_im
    cfr = ((lbr - 1.0) * lam_re + lbi * lam_im) / den
    cfi = (lbi * lam_re - (lbr - 1.0) * lam_im) / den
    bbr = cfr[..., None] * b_re - cfi[..., None] * b_im
    bbi = cfr[..., None] * b_im + cfi[..., None] * b_re
    nj = SSM_GROUPS // SSM_GB
    eye = jnp.eye(SSM_GB, dtype=F32)

    def bmat(bb):
        bb = bb.reshape(nj, SSM_GB, SSM_STATE, SSM_GROUP)
        m = jnp.einsum("ab,jbph->jahbp", eye, bb)
        return m.reshape(nj, SSM_GB * SSM_GROUP, SSM_GB * SSM_STATE)

    def cmat(cc):
        cc = cc.reshape(nj, SSM_GB, SSM_GROUP, SSM_STATE)
        m = jnp.einsum("ab,jbhp->japbh", eye, cc)
        return m.reshape(nj, SSM_GB * SSM_STATE, SSM_GB * SSM_GROUP)

    b_mat = jnp.concatenate([bmat(bbr), bmat(bbi)], axis=-1).astype(BF16)
    c_mat = jnp.concatenate([cmat(c_re), -cmat(c_im)], axis=1).astype(BF16)

    def power(kk):
        m = jnp.exp(kk * ar[None])
        return ((m * jnp.cos(kk * ai[None])).reshape(-1, SSM_LANES),
                (m * jnp.sin(kk * ai[None])).reshape(-1, SSM_LANES))

    rows = jnp.arange(SUBLANES, dtype=F32)[:, None]
    shifts = []
    for s in (1, 2, 4):
        pr, pi = power(jnp.full((1, 1, 1), float(s), F32))
        keep = rows >= s
        shifts += [jnp.where(keep, pr, 0.0), jnp.where(keep, pi, 0.0)]
    pr, pi = power(jnp.arange(1, SUBLANES + 1, dtype=F32)[:, None, None])
    lam_tab = jnp.stack(shifts + [pr, pi])
    return b_mat, c_mat, lam_tab


def _ssm_kernel(u_ref, bm_ref, cm_ref, lt_ref, d_ref, gw_ref, gb_ref, o_ref, xr_ref, xi_ref, st_ref):
    tb = u_ref.shape[0]
    nj = bm_ref.shape[0]
    gw = SSM_GB * SSM_GROUP
    sw = SSM_GB * SSM_STATE

    @pl.when(pl.program_id(1) == 0)
    def _():
        st_ref[...] = jnp.zeros_like(st_ref)

    u = u_ref[...]
    ub = u.astype(BF16)
    for j in range(nj):
        bu = _dot(ub[:, j * gw:(j + 1) * gw], bm_ref[j])
        xr_ref[:, j * sw:(j + 1) * sw] = bu[:, :sw]
        xi_ref[:, j * sw:(j + 1) * sw] = bu[:, sw:]

    def row_body(r, carry):
        rs = pl.ds(pl.multiple_of(r * SUBLANES, SUBLANES), SUBLANES)
        for c in range(SSM_LANES // SSM_LC):
            sl = slice(c * SSM_LC, (c + 1) * SSM_LC)
            xr = xr_ref[rs, sl]
            xi = xi_ref[rs, sl]
            for k, s in enumerate((1, 2, 4)):
                lr, li = lt_ref[2 * k, :, sl], lt_ref[2 * k + 1, :, sl]
                sr = pltpu.roll(xr, s, axis=0)
                si = pltpu.roll(xi, s, axis=0)
                xr, xi = xr + (lr * sr - li * si), xi + (lr * si + li * sr)
            pr, pi = lt_ref[6, :, sl], lt_ref[7, :, sl]
            cr, ci = st_ref[0:1, sl], st_ref[1:2, sl]
            xr, xi = xr + (pr * cr - pi * ci), xi + (pr * ci + pi * cr)
            xr_ref[rs, sl] = xr
            xi_ref[rs, sl] = xi
            st_ref[0:1, sl] = xr[SUBLANES - 1:, :]
            st_ref[1:2, sl] = xi[SUBLANES - 1:, :]
        return carry

    lax.fori_loop(0, tb // SUBLANES, row_body, 0)

    ys = []
    for j in range(nj):
        xs = jnp.concatenate([xr_ref[:, j * sw:(j + 1) * sw], xi_ref[:, j * sw:(j + 1) * sw]], axis=-1)
        ys.append(_dot(xs.astype(BF16), cm_ref[j]))
    y = jnp.concatenate(ys, axis=-1) + d_ref[...] * u
    y = jax.nn.gelu(y)
    gate = _dot(y.astype(BF16), gw_ref[...]) + gb_ref[...]
    o_ref[...] = (y * jax.nn.sigmoid(gate)).astype(o_ref.dtype)


def _ssm(u, b_mat, c_mat, lam_tab, d_skip, glu_w, glu_b, nbatch):
    t, w = u.shape
    per_b = t // nbatch // TB_SSM
    full = lambda a: pl.BlockSpec(a.shape, lambda b, i: (0,) * a.ndim)
    d_row = d_skip.reshape(1, w)
    gb_row = glu_b.reshape(1, w)
    return pl.pallas_call(
        _ssm_kernel,
        out_shape=jax.ShapeDtypeStruct((t, w), BF16),
        grid=(nbatch, per_b),
        in_specs=[
            pl.BlockSpec((TB_SSM, w), lambda b, i: (b * per_b + i, 0)),
            full(b_mat), full(c_mat), full(lam_tab), full(d_row), full(glu_w), full(gb_row),
        ],
        out_specs=pl.BlockSpec((TB_SSM, w), lambda b, i: (b * per_b + i, 0)),
        scratch_shapes=[
            pltpu.VMEM((TB_SSM, SSM_LANES), F32),
            pltpu.VMEM((TB_SSM, SSM_LANES), F32),
            pltpu.VMEM((2, SSM_LANES), F32),
        ],
        compiler_params=_cparams(("arbitrary", "arbitrary")),
        name="s5_mixer",
    )(u, b_mat, c_mat, lam_tab, d_row, glu_w, gb_row)


def _log_sigmoid(z):
    return jnp.minimum(z, 0.0) - jnp.log1p(jnp.exp(-jnp.abs(z)))


def _silu(x):
    return x * jax.nn.sigmoid(x)


def _gla_kernel(qk_ref, v_ref, g_ref, a_ref, wa_ref, ba_ref, gain_ref, o_ref, st_ref):
    tb = qk_ref.shape[0]
    ck = GLA_CHUNK

    @pl.when(pl.program_id(1) == 0)
    def _():
        st_ref[...] = jnp.zeros_like(st_ref)

    log_a = _log_sigmoid(_dot3(a_ref[:, :GLA_RANK], wa_ref[...]) + ba_ref[...]) * (1.0 / GLA_TAU)
    ri = lax.broadcasted_iota(I32, (ck, ck), 0)
    ci = lax.broadcasted_iota(I32, (ck, ck), 1)
    causal = ri >= ci
    tri = jnp.where(causal, 1.0, 0.0).astype(BF16)
    for c in range(tb // ck):
        rows = slice(c * ck, (c + 1) * ck)
        la_hi, la_lo = _split_bf16(log_a[rows])
        dec = _dot(tri, la_hi) + _dot(tri, la_lo)
        for h in range(GLA_HEADS):
            kcol = slice(h * GLA_DK, (h + 1) * GLA_DK)
            vcol = slice(h * GLA_DV, (h + 1) * GLA_DV)
            b = dec[:, kcol]
            b_last = b[ck - 1:ck, :]
            q = qk_ref[rows, kcol] * (GLA_DK ** -0.5)
            k = qk_ref[rows, GLA_QK + h * GLA_DK:GLA_QK + (h + 1) * GLA_DK]
            v = v_ref[rows, vcol].astype(BF16)
            q_dec = (q * jnp.exp(b)).astype(BF16)
            k_intra = (k * jnp.exp(-b)).astype(BF16)
            k_state = (k * jnp.exp(b_last - b)).astype(BF16)
            scores = jnp.where(causal, _dot_nt(q_dec, k_intra), 0.0)
            st = st_ref[h]
            o = _dot(scores.astype(BF16), v) + _dot_nt(q_dec, st.astype(BF16))
            st_ref[h] = st * jnp.exp(b_last) + _dot_tn(v, k_state)
            o = _rms(o) * gain_ref[:, vcol] * _silu(g_ref[rows, vcol])
            o_ref[rows, vcol] = o.astype(o_ref.dtype)


def _gla(gqk, gv, gg, ga, wa, ba, gain, nbatch):
    t = gqk.shape[0]
    per_b = t // nbatch // TB_GLA
    row = lambda w: pl.BlockSpec((TB_GLA, w), lambda b, i: (b * per_b + i, 0))
    full = lambda a: pl.BlockSpec(a.shape, lambda b, i: (0,) * a.ndim)
    ba_row = ba.reshape(1, GLA_QK)
    gain_row = gain.reshape(1, GLA_WIDTH)
    return pl.pallas_call(
        _gla_kernel,
        out_shape=jax.ShapeDtypeStruct((t, GLA_WIDTH), BF16),
        grid=(nbatch, per_b),
        in_specs=[row(2 * GLA_QK), row(GLA_WIDTH), row(GLA_WIDTH), row(LANES),
                  full(wa), full(ba_row), full(gain_row)],
        out_specs=row(GLA_WIDTH),
        scratch_shapes=[pltpu.VMEM((GLA_HEADS, GLA_DV, GLA_DK), F32)],
        compiler_params=_cparams(("arbitrary", "arbitrary")),
        name="gla_mixer",
    )(gqk, gv, gg, ga, wa, ba_row, gain_row)


def _ret_kernel(qk_ref, v_ref, g_ref, pos_ref, inv_ref, gain_ref, o_ref, st_ref):
    tb = qk_ref.shape[0]

    @pl.when(pl.program_id(1) == 0)
    def _():
        st_ref[...] = jnp.zeros_like(st_ref)

    ang = pos_ref[...].astype(F32) * inv_ref[...]
    lane = lax.broadcasted_iota(I32, (1, RET_DK), 1)
    cos = jnp.cos(ang)
    sin = jnp.sin(ang) * jnp.where(lane < RET_DK // 2, -1.0, 1.0)
    ri = lax.broadcasted_iota(I32, (tb, tb), 0)
    ci = lax.broadcasted_iota(I32, (tb, tb), 1)
    rel = (ri - ci).astype(F32)
    idx = lax.broadcasted_iota(I32, (tb, 1), 0).astype(F32)
    for h in range(RET_HEADS):
        log_gamma = math.log1p(-(2.0 ** (-5.0 - h)))
        kcol = slice(h * RET_DK, (h + 1) * RET_DK)
        vcol = slice(h * RET_DV, (h + 1) * RET_DV)
        q = qk_ref[:, kcol]
        k = qk_ref[:, RET_QK + h * RET_DK:RET_QK + (h + 1) * RET_DK]
        q = (q * cos + pltpu.roll(q, RET_DK // 2, axis=1) * sin) * (RET_DK ** -0.5)
        k = k * cos + pltpu.roll(k, RET_DK // 2, axis=1) * sin
        v = v_ref[:, vcol].astype(BF16)
        decay = jnp.where(rel >= 0, jnp.exp(jnp.maximum(rel, 0.0) * log_gamma), 0.0)
        scores = _dot_nt(q.astype(BF16), k.astype(BF16)) * decay
        q_w = jnp.exp((idx + 1.0) * log_gamma)
        k_w = jnp.exp((tb - 1.0 - idx) * log_gamma)
        st = st_ref[h]
        o = _dot(scores.astype(BF16), v) + _dot((q * q_w).astype(BF16), st.astype(BF16))
        st_ref[h] = st * math.exp(tb * log_gamma) + _dot_tn((k * k_w).astype(BF16), v)
        o = _rms(o) * gain_ref[:, vcol] * _silu(g_ref[:, vcol])
        o_ref[:, vcol] = o.astype(o_ref.dtype)


def _ret(rqk, rv, rg, pos, inv, gain, nbatch):
    t = rqk.shape[0]
    per_b = t // nbatch // TB_RET
    row = lambda w: pl.BlockSpec((TB_RET, w), lambda b, i: (b * per_b + i, 0))
    full = lambda a: pl.BlockSpec(a.shape, lambda b, i: (0,) * a.ndim)
    gain_row = gain.reshape(1, RET_WIDTH)
    return pl.pallas_call(
        _ret_kernel,
        out_shape=jax.ShapeDtypeStruct((t, RET_WIDTH), BF16),
        grid=(nbatch, per_b),
        in_specs=[row(2 * RET_QK), row(RET_WIDTH), row(RET_WIDTH), row(1), full(inv), full(gain_row)],
        out_specs=row(RET_WIDTH),
        scratch_shapes=[pltpu.VMEM((RET_HEADS, RET_DK, RET_DV), F32)],
        compiler_params=_cparams(("arbitrary", "arbitrary")),
        name="ret_mixer",
    )(rqk, rv, rg, pos, inv, gain_row)


def _mix_post_kernel(y_ref, x_ref, mod_ref, gpost_ref, gpre_ref, rwh_ref, rwl_ref, rb_ref,
                     xo_ref, ho_ref, ti_ref, tw_ref):
    m = mod_ref[...]
    x = x_ref[...] + m[2:3] * (_rms(y_ref[...]) * gpost_ref[...])
    xo_ref[...] = x
    h = _rms(x) * gpre_ref[...] * (1.0 + m[4:5]) + m[3:4]
    ho_ref[...] = h
    hh, hl = _split_bf16(h)
    logits = _dot(hh, rwh_ref[...]) + _dot(hl, rwh_ref[...]) + _dot(hh, rwl_ref[...]) + rb_ref[...]
    lane = lax.broadcasted_iota(I32, logits.shape, 1)
    vals, idxs = [], []
    for _ in range(TOP_K):
        mx = jnp.max(logits, axis=-1, keepdims=True)
        ix = jnp.min(jnp.where(logits == mx, lane, LANES), axis=-1, keepdims=True)
        vals.append(mx)
        idxs.append(ix)
        logits = jnp.where(lane == ix, -jnp.inf, logits)
    es = [jnp.exp(v - vals[0]) for v in vals]
    inv_sum = 1.0 / (es[0] + es[1] + es[2] + es[3])
    ti = jnp.zeros(lane.shape, I32)
    tw = jnp.zeros(lane.shape, F32)
    for kk in range(TOP_K):
        ti = jnp.where(lane == kk, idxs[kk], ti)
        tw = jnp.where(lane == kk, es[kk] * inv_sum, tw)
    ti_ref[...] = ti
    tw_ref[...] = tw


def _mix_post(y, x2, mod6, gpost, gpre, rw_hi, rw_lo, rb, seq):
    t, d = x2.shape
    per_b = seq // TM_ROW
    row = lambda w: pl.BlockSpec((TM_ROW, w), lambda i: (i, 0))
    full = lambda a: pl.BlockSpec(a.shape, lambda i: (0,) * a.ndim)
    gpost, gpre = gpost.reshape(1, d), gpre.reshape(1, d)
    return pl.pallas_call(
        _mix_post_kernel,
        out_shape=(jax.ShapeDtypeStruct((t, d), F32), jax.ShapeDtypeStruct((t, d), F32),
                   jax.ShapeDtypeStruct((t, LANES), I32), jax.ShapeDtypeStruct((t, LANES), F32)),
        grid=(t // TM_ROW,),
        in_specs=[row(d), row(d), pl.BlockSpec((None, N_MOD, d), lambda i: (i // per_b, 0, 0)),
                  full(gpost), full(gpre), full(rw_hi), full(rw_lo), full(rb)],
        out_specs=(row(d), row(d), row(LANES), row(LANES)),
        compiler_params=_cparams(("arbitrary",)),
        name="mix_post_router",
    )(y, x2, mod6, gpost, gpre, rw_hi, rw_lo, rb)


def _route(top_idx, n_tiles):
    t = top_idx.shape[0]
    assert math.gcd(ROUTE_STRIDE, t) == 1
    perm = (jnp.arange(t, dtype=I32) * ROUTE_STRIDE) % t
    inv_perm = (jnp.arange(t, dtype=I32) * pow(ROUTE_STRIDE, -1, t)) % t
    e_flat = top_idx[perm].reshape(-1)
    onehot =e_flat[:, None] == jnp.arange(N_EXPERTS, dtype=I32)[None, :]
    blk = 256
    nblk = (t * TOP_K) // blk
    oh3 = onehot.astype(BF16).reshape(nblk, blk, N_EXPERTS)
    within = jnp.einsum("ij,bjk->bik", jnp.tril(jnp.ones((blk, blk), BF16)), oh3, preferred_element_type=F32)
    tot = within[:, -1, :]
    before = jnp.dot(jnp.tril(jnp.ones((nblk, nblk), BF16), -1), tot.astype(BF16), preferred_element_type=F32)
    csum = (within + before[:, None, :]).reshape(t * TOP_K, N_EXPERTS)
    counts = (before[-1] + tot[-1]).astype(I32)
    padded = ((counts + TM_MOE - 1) // TM_MOE) * TM_MOE
    ends = jnp.cumsum(padded)
    starts = ends - padded
    dest = jnp.sum(jnp.where(onehot, csum - 1.0 + starts.astype(F32)[None, :], 0.0), axis=1).astype(I32)
    row_src = (jnp.arange(n_tiles * TM_MOE, dtype=I32) % t).at[dest].set(jnp.repeat(perm, TOP_K))
    n_used = ends[-1] // TM_MOE
    tile_ids = jnp.minimum(jnp.arange(n_tiles, dtype=I32), n_used - 1)
    tile_expert = jnp.sum(ends[None, :] <= (tile_ids * TM_MOE)[:, None], axis=1).astype(I32)
    pos_t = dest.reshape(t, TOP_K)[inv_perm].T.reshape(-1)
    return row_src, tile_expert, n_used.reshape(1).astype(I32), pos_t


def _new_weights(te_ref, m):
    prev = te_ref[jnp.maximum(m - 1, 0)]
    return jnp.logical_or(m == 0, te_ref[m] != prev)


def _gu_kernel(te_ref, nu_ref, src_ref, h_ref, wg_ref, wu_ref, bg_ref, bu_ref, o_ref,
               wcat_ref, xa_ref, xb_ref, sem, *, unrolled_issue):
    m = pl.program_id(1)
    nt = pl.num_programs(1)
    step = pl.program_id(0) * nt + m
    n_steps = pl.num_programs(0) * nt
    next_tile = jnp.where(m + 1 < nt, m + 1, 0)
    bufs = (xa_ref, xb_ref)

    def row_copy(src_row, slot, r):
        return pltpu.make_async_copy(h_ref.at[pl.ds(src_row, 1)], bufs[slot].at[pl.ds(r, 1)], sem.at[slot])

    def issue(tile, slot, unrolled):
        base = tile * TM_MOE
        if unrolled:
            for r in range(TM_MOE):
                row_copy(src_ref[base + r], slot, r).start()
        else:
            def body(r, carry):
                row_copy(src_ref[base + r], slot, r).start()
                return carry
            lax.fori_loop(0, TM_MOE, body, 0, unroll=8)

    def wait(slot):
        pltpu.make_async_copy(h_ref.at[pl.ds(0, TM_MOE)], bufs[slot], sem.at[slot]).wait()

    @pl.when(step == 0)
    def _():
        issue(0, 0, False)

    @pl.when(_new_weights(te_ref, m))
    def _():
        wcat_ref[:, 0:TN_GU] = wg_ref[...].astype(BF16)
        wcat_ref[:, TN_GU:] = wu_ref[...].astype(BF16)

    valid = m < nu_ref[0]
    for slot in range(2):
        mine = step % 2 == slot

        @pl.when(jnp.logical_and(mine, valid))
        def _(slot=slot):
            wait(slot)
            issue(next_tile, 1 - slot, unrolled_issue)
            x = bufs[slot][...].astype(BF16)
            gu = _dot(x, wcat_ref[...])
            gate = gu[:, 0:TN_GU] + bg_ref[...]
            up = gu[:, TN_GU:] + bu_ref[...]
            gate = jnp.minimum(gate, SWIGLU_LIMIT)
            up = jnp.clip(up, -SWIGLU_LIMIT, SWIGLU_LIMIT)
            o_ref[...] = ((up + 1.0) * gate * jax.nn.sigmoid(SWIGLU_ALPHA * gate)).astype(o_ref.dtype)

        @pl.when(jnp.logical_and(mine, jnp.logical_not(valid)))
        def _(slot=slot):
            wait(slot)
            issue(next_tile, 1 - slot, False)
            o_ref[...] = jnp.zeros_like(o_ref)

    for slot in range(2):
        @pl.when(jnp.logical_and(step == n_steps - 1, step % 2 == 1 - slot))
        def _(slot=slot):
            wait(slot)


def _expert_gu(tile_expert, n_used, row_src, h, w_gu, b_gu4, layer, *, unrolled_issue=True):
    mrows = row_src.shape[0]
    nt = mrows // TM_MOE
    nn = D_EXPERT // TN_GU
    wspec = lambda off: pl.BlockSpec((None, None, D_MODEL, TN_GU),
                                     lambda n, m, te, nu, src: (layer, te[m], 0, off + n))
    bspec = lambda off: pl.BlockSpec((None, None, 1, TN_GU), lambda n, m, te, nu, src: (layer, te[m], 0, off + n))
    return pl.pallas_call(
        functools.partial(_gu_kernel, unrolled_issue=unrolled_issue),
        out_shape=jax.ShapeDtypeStruct((mrows, D_EXPERT), BF16),
        grid_spec=pltpu.PrefetchScalarGridSpec(
            num_scalar_prefetch=3,
            grid=(nn, nt),
            in_specs=[pl.BlockSpec(memory_space=pl.ANY), wspec(0), wspec(nn), bspec(0), bspec(nn)],
            out_specs=pl.BlockSpec((TM_MOE, TN_GU), lambda n, m, te, nu, src: (m, n)),
            scratch_shapes=[pltpu.VMEM((D_MODEL, 2 * TN_GU), BF16),
                            pltpu.VMEM((TM_MOE, D_MODEL), F32), pltpu.VMEM((TM_MOE, D_MODEL), F32),
                            pltpu.SemaphoreType.DMA((2,))],
        ),
        compiler_params=_cparams(("arbitrary", "arbitrary")),
        name="moe_gate_up",
    )(tile_expert, n_used, row_src, h, w_gu, w_gu, b_gu4, b_gu4)


def _down_kernel(te_ref, nu_ref, a_ref, w_ref, b_ref, o_ref, wb_ref):
    m = pl.program_id(1)

    @pl.when(_new_weights(te_ref, m))
    def _():
        wb_ref[...] = w_ref[...].astype(BF16)

    @pl.when(m < nu_ref[0])
    def _():
        o_ref[...] = _dot(a_ref[...], wb_ref[...]) + b_ref[...]

    @pl.when(m >= nu_ref[0])
    def _():
        o_ref[...] = jnp.zeros_like(o_ref)


def _expert_down(tile_expert, n_used, act, w_down, b_down4, layer):
    mrows = act.shape[0]
    nt = mrows // TM_MOE
    return pl.pallas_call(
        _down_kernel,
        out_shape=jax.ShapeDtypeStruct((mrows, D_MODEL), F32),
        grid_spec=pltpu.PrefetchScalarGridSpec(
            num_scalar_prefetch=2,
            grid=(D_MODEL // TN_DOWN, nt),
            in_specs=[
                pl.BlockSpec((TM_MOE, D_EXPERT), lambda n, m, te, nu: (jnp.minimum(m, nu[0] - 1), 0)),
                pl.BlockSpec((None, None, D_EXPERT, TN_DOWN), lambda n, m, te, nu: (layer, te[m], 0, n)),
                pl.BlockSpec((None, None, 1, TN_DOWN), lambda n, m, te, nu: (layer, te[m], 0, n)),
            ],
            out_specs=pl.BlockSpec((TM_MOE, TN_DOWN), lambda n, m, te, nu: (m, n)),
            scratch_shapes=[pltpu.VMEM((D_EXPERT, TN_DOWN), BF16)],
        ),
        compiler_params=_cparams(("arbitrary", "arbitrary")),
        name="moe_down",
    )(tile_expert, n_used, act, w_down, b_down4)


def _combine_kernel(pos_ref, ys_ref, tw_ref, x_ref, mod_ref, gain_ref, o_ref, buf_ref, sem):
    i = pl.program_id(0)
    n = pl.num_programs(0)
    tm = x_ref.shape[0]
    t_total = n * tm
    rows = TOP_K * tm

    def issue(tile, slot, unrolled):
        for kk in range(TOP_K):
            base = kk * t_total + tile * tm

            def start(r, kk=kk, base=base):
                pltpu.make_async_copy(ys_ref.at[pl.ds(pos_ref[base + r], 1)],
                                      buf_ref.at[slot, pl.ds(kk * tm + r, 1)], sem.at[slot]).start()

            if unrolled:
                for r in range(tm):
                    start(r)
            else:
                def body(r, carry, start=start):
                    start(r)
                    return carry
                lax.fori_loop(0, tm, body, 0, unroll=8)

    @pl.when(i == 0)
    def _():
        issue(0, 0, False)

    for slot in range(2):
        @pl.when(i % 2 == slot)
        def _(slot=slot):
            @pl.when(i + 1 < n)
            def _():
                issue(i + 1, 1 - slot, True)

            pltpu.make_async_copy(ys_ref.at[pl.ds(0, rows)], buf_ref.at[slot], sem.at[slot]).wait()
            tw = tw_ref[...]
            y = tw[:, 0:1] * buf_ref[slot, 0:tm, :]
            for kk in range(1, TOP_K):
                y += tw[:, kk:kk + 1] * buf_ref[slot, kk * tm:(kk + 1) * tm, :]
            m = mod_ref[...]
            o_ref[...] = x_ref[...] + m[5:6] * (_rms(y) * gain_ref[...])


def _combine(pos_t, ys, top_w, x2, mod6, gain, seq):
    t, d = x2.shape
    per_b = seq // TM_CMB
    gain = gain.reshape(1, d)
    return pl.pallas_call(
        _combine_kernel,
        out_shape=jax.ShapeDtypeStruct((t, d), F32),
        grid_spec=pltpu.PrefetchScalarGridSpec(
            num_scalar_prefetch=1,
            grid=(t // TM_CMB,),
            in_specs=[
                pl.BlockSpec(memory_space=pl.ANY),
                pl.BlockSpec((TM_CMB, LANES), lambda i, pos: (i, 0)),
                pl.BlockSpec((TM_CMB, d), lambda i, pos: (i, 0)),
                pl.BlockSpec((None, N_MOD, d), lambda i, pos: (i // per_b, 0, 0)),
                pl.BlockSpec((1, d), lambda i, pos: (0, 0)),
            ],
            out_specs=pl.BlockSpec((TM_CMB, d), lambda i, pos: (i, 0)),
            scratch_shapes=[pltpu.VMEM((2, TOP_K * TM_CMB, d), F32), pltpu.SemaphoreType.DMA((2,))],
        ),
        compiler_params=_cparams(("arbitrary",)),
        name="moe_combine",
    )(pos_t, ys, top_w, x2, mod6, gain)


def kernel(x, c, positions, ada_w, ada_b, mix_pre_gain, mix_post_gain, ffn_pre_gain, ffn_post_gain,
           w_in, w_out, ssm_lam_re, ssm_lam_im, ssm_log_step, ssm_b_re, ssm_b_im, ssm_c_re, ssm_c_im,
           ssm_d, ssm_glu_w, ssm_glu_b, gla_wa, gla_ba, gla_norm_gain, ret_norm_gain,
           router_w, router_b, exp_w_gu, exp_b_gu, exp_w_down, exp_b_down):
    nbatch, seq, d = x.shape
    t = nbatch * seq
    depth = ada_w.shape[0]
    n_tiles = (t * TOP_K) // TM_MOE + N_EXPERTS

    x2 = x.reshape(t, d)
    pos = positions.reshape(t, 1)
    half = RET_DK // 2
    inv = jnp.power(ROPE_BASE, -jnp.arange(half, dtype=F32) / half)
    inv = jnp.concatenate([inv, inv]).reshape(1, RET_DK)
    mod = _ada(c, ada_w, ada_b)

    rw =jnp.pad(router_w, ((0, 0), (0, 0), (0, LANES - N_EXPERTS)))
    rw_hi = rw.astype(BF16)
    rw_lo = (rw - rw_hi.astype(F32)).astype(BF16)
    rb = jnp.pad(router_b, ((0, 0), (0, LANES - N_EXPERTS)), constant_values=-1e30)
    b_gu4 = exp_b_gu.reshape(depth, N_EXPERTS, 1, 2 * D_EXPERT)
    b_down4 = exp_b_down.reshape(depth, N_EXPERTS, 1, d)
    glu_w = ssm_glu_w.astype(BF16)
    w_in_t = jnp.swapaxes(w_in, 1, 2)

    for l in range(depth):
        mod6 = mod[l].reshape(nbatch, N_MOD, d)
        h = _prenorm(x2, mix_pre_gain[l], mod6, seq, shift_i=0, scale_i=1)
        u = _mm(h, w_in_t, l, OFF_U, SSM_WIDTH)
        gqk = _mm(h, w_in_t, l, OFF_GQK, 2 * GLA_QK)
        gv = _mm(h, w_in_t, l, OFF_GV, GLA_WIDTH)
        gg = _mm(h, w_in_t, l, OFF_GG, GLA_WIDTH)
        ga = _mm(h, w_in_t, l, OFF_GA, LANES, tn=LANES)
        rqk = _mm_shift(h, w_in_t, l, OFF_TAIL, 2 * RET_QK)
        rv = _mm_shift(h, w_in_t, l, OFF_TAIL + 2 * RET_QK, RET_WIDTH)
        rg = _mm_shift(h, w_in_t, l, OFF_TAIL + 2 * RET_QK + RET_WIDTH, RET_WIDTH)

        b_mat, c_mat, lam_tab = _ssm_params(ssm_lam_re[l], ssm_lam_im[l], ssm_log_step[l], ssm_b_re[l],
                                            ssm_b_im[l], ssm_c_re[l], ssm_c_im[l])
        y_ssm = _ssm(u, b_mat, c_mat, lam_tab, ssm_d[l], glu_w[l], ssm_glu_b[l], nbatch)
        y_gla = _gla(gqk, gv, gg, ga, gla_wa[l], gla_ba[l], gla_norm_gain[l], nbatch)
        y_ret = _ret(rqk, rv, rg, pos, inv, ret_norm_gain[l], nbatch)
        y = _mm_out(y_ssm, y_gla, y_ret, w_out, l)

        x2, h2, top_idx, top_w = _mix_post(y, x2, mod6, mix_post_gain[l], ffn_pre_gain[l],
                                           rw_hi[l], rw_lo[l], rb[l:l + 1], seq)
        row_src, tile_expert, n_used, pos_t = _route(top_idx[:, :TOP_K], n_tiles)
        act = _expert_gu(tile_expert, n_used, row_src, h2, exp_w_gu, b_gu4, l)
        ys = _expert_down(tile_expert, n_used, act, exp_w_down, b_down4, l)
        x2 = _combine(pos_t, ys, top_w, x2, mod6, ffn_post_gain[l], seq)
    return x2.reshape(nbatch, seq, d)
```

```python
import functools
import math

import jax
import jax.numpy as jnp
from jax import lax
from jax.experimental import pallas as pl
from jax.experimental.pallas import tpu as pltpu

F32 = jnp.float32
BF16 = jnp.bfloat16
I32 = jnp.int32

D_MODEL = 4096
N_MOD = 6
SSM_WIDTH = 1024
SSM_GROUP = 16
SSM_GROUPS = 64
SSM_STATE = 64
SSM_LANES = SSM_GROUPS * SSM_STATE
GLA_WIDTH = 1536
GLA_HEADS = 4
GLA_DV = 384
GLA_DK = 192
GLA_QK = 768
GLA_RANK = 16
GLA_TAU = 16.0
GLA_CHUNK = 64
RET_WIDTH = 1536
RET_HEADS = 6
RET_DV = 256
RET_DK = 128
RET_QK = 768
ROPE_BASE = 10000.0
N_EXPERTS = 32
TOP_K = 4
D_EXPERT = 768
SWIGLU_LIMIT = 7.0
SWIGLU_ALPHA = 1.702
EPS = 1e-6
OFF_U = 0
OFF_GQK = 1024
OFF_GV = 2560
OFF_GG = 4096
OFF_GA = 5632
OFF_TAIL = 5648

LANES = 128
SUBLANES = 8
VMEM_LIMIT = 52 * 1024 * 1024

TM_ROW = 256
TM_MM = 1024
TN_MM = 512
TB_SSM = 256
SSM_LC = 512
SSM_GB = 16
TB_GLA = 256
TB_RET = 256
TM_MOE = 256
TN_GU = 384
TN_DOWN = 4096
TM_CMB = 128
ROUTE_STRIDE = 2053


def _cparams(sem):
    return pltpu.CompilerParams(dimension_semantics=sem, vmem_limit_bytes=VMEM_LIMIT)


def _dot(a, b):
    return jnp.dot(a, b, preferred_element_type=F32)


def _dot_nt(a, b):
    return lax.dot_general(a, b, (((1,), (1,)), ((), ())), preferred_element_type=F32)


def _dot_tn(a, b):
    return lax.dot_general(a, b, (((0,), (0,)), ((), ())), preferred_element_type=F32)


def _split_bf16(x):
    hi = x.astype(BF16)
    lo = (x - hi.astype(F32)).astype(BF16)
    return hi, lo


def _dot3(a, b):
    ah, al = _split_bf16(a)
    bh, bl = _split_bf16(b)
    return _dot(ah, bh) + _dot(al, bh) + _dot(ah, bl)


def _ada_kernel(c_ref, w_ref, b_ref, o_ref, cab_ref):
    nb = c_ref.shape[0]
    d, tn = w_ref.shape
    rk = 64

    @pl.when(jnp.logical_and(pl.program_id(0) == 0, pl.program_id(1) == 0))
    def _():
        for b in range(nb):
            cc = c_ref[b]
            cab_ref[b] = jnp.broadcast_to(cc * jax.nn.sigmoid(cc), (d, LANES))

    def body(i, accs):
        r0 = pl.multiple_of(i * rk, rk)
        wk = w_ref[pl.ds(r0, rk), :]
        out = []
        for b in range(nb):
            ca = cab_ref[b, pl.ds(r0, rk), :]
            prod = jnp.concatenate([wk[:, q * LANES:(q + 1) * LANES] * ca for q in range(tn // LANES)], axis=1)
            out.append(accs[b] + jnp.sum(prod.reshape(rk // SUBLANES, SUBLANES, tn), axis=0))
        return tuple(out)

    accs = lax.fori_loop(0, d // rk, body, tuple(jnp.zeros((SUBLANES, tn), F32) for _ in range(nb)))
    for b in range(nb):
        o_ref[b:b + 1, :] = jnp.sum(accs[b], axis=0, keepdims=True) + b_ref[...]


def _ada(c, ada_w, ada_b):
    n_layers, d, n = ada_w.shape
    nb = c.shape[0]
    tn = 512
    return pl.pallas_call(
        _ada_kernel,
        out_shape=jax.ShapeDtypeStruct((n_layers, nb, n), F32),
        grid=(n_layers, n // tn),
        in_specs=[
            pl.BlockSpec((nb, d, 1), lambda l, j: (0, 0, 0)),
            pl.BlockSpec((None, d, tn), lambda l, j: (l, 0, j)),
            pl.BlockSpec((None, 1, tn), lambda l, j: (l, 0, j)),
        ],
        out_specs=pl.BlockSpec((None, nb, tn), lambda l, j: (l, 0, j)),
        scratch_shapes=[pltpu.VMEM((nb, d, LANES), F32)],
        compiler_params=_cparams(("arbitrary", "arbitrary")),
        name="ada_mod",
    )(c.reshape(nb, d, 1), ada_w, ada_b.reshape(n_layers, 1, n))


def _rms(x):
    return x * lax.rsqrt(jnp.mean(x * x, axis=-1, keepdims=True) + EPS)


def _prenorm_kernel(x_ref, g_ref, mod_ref, o_ref, *, shift_i, scale_i):
    m = mod_ref[...]
    h = _rms(x_ref[...]) * g_ref[...]
    o_ref[...] = (h * (1.0 + m[scale_i:scale_i + 1]) + m[shift_i:shift_i + 1]).astype(o_ref.dtype)


def _prenorm(x2, gain, mod6, seq, *, shift_i, scale_i):
    t, d = x2.shape
    per_b = seq // TM_ROW
    return pl.pallas_call(
        functools.partial(_prenorm_kernel, shift_i=shift_i, scale_i=scale_i),
        out_shape=jax.ShapeDtypeStruct((t, d), BF16),
        grid=(t // TM_ROW,),
        in_specs=[
            pl.BlockSpec((TM_ROW, d), lambda i: (i, 0)),
            pl.BlockSpec((1, d), lambda i: (0, 0)),
            pl.BlockSpec((None, N_MOD, d), lambda i: (i // per_b, 0, 0)),
        ],
        out_specs=pl.BlockSpec((TM_ROW, d), lambda i: (i, 0)),
        compiler_params=_cparams(("arbitrary",)),
        name="prenorm",
    )(x2, gain.reshape(1, d), mod6)


def _mm_kernel(x_ref, w_ref, o_ref, wbf_ref):
    @pl.when(pl.program_id(1) == 0)
    def _():
        wbf_ref[...] = w_ref[...].astype(BF16)

    o_ref[...] = _dot_nt(x_ref[...], wbf_ref[...]).astype(o_ref.dtype)


def _mm(x, w_t, layer, row_off, width, *, tn=TN_MM, out_dtype=F32):
    t, k = x.shape
    jb = row_off // tn
    return pl.pallas_call(
        _mm_kernel,
        out_shape=jax.ShapeDtypeStruct((t, width), out_dtype),
        grid=(width // tn, t // TM_MM),
        in_specs=[
            pl.BlockSpec((TM_MM, k), lambda j, i: (i, 0)),
            pl.BlockSpec((None, tn, k), lambda j, i: (layer, jb + j, 0)),
        ],
        out_specs=pl.BlockSpec((TM_MM, tn), lambda j, i: (i, j)),
        scratch_shapes=[pltpu.VMEM((tn, k), BF16)],
        compiler_params=_cparams(("arbitrary", "arbitrary")),
        name="proj_in",
    )(x, w_t)


def _mm_shift_kernel(x_ref, wa_ref, wb_ref, o_ref, wbf_ref, *, shift):
    @pl.when(pl.program_id(1) == 0)
    def _():
        tn = wa_ref.shape[0]
        wbf_ref[0:tn - shift, :] = wa_ref[shift:, :].astype(BF16)
        wbf_ref[tn - shift:, :] = wb_ref[...].astype(BF16)

    o_ref[...] = _dot_nt(x_ref[...], wbf_ref[...]).astype(o_ref.dtype)


def _mm_shift(x, w_t, layer, row_off, width, *, out_dtype=F32):
    t, k = x.shape
    tn = TN_MM
    jb, shift = divmod(row_off, tn)
    assert shift % 16 == 0 and tn % shift == 0
    return pl.pallas_call(
        functools.partial(_mm_shift_kernel, shift=shift),
        out_shape=jax.ShapeDtypeStruct((t, width), out_dtype),
        grid=(width // tn, t // TM_MM),
        in_specs=[
            pl.BlockSpec((TM_MM, k), lambda j, i: (i, 0)),
            pl.BlockSpec((None, tn, k), lambda j, i: (layer, jb + j, 0)),
            pl.BlockSpec((None, shift, k), lambda j, i: (layer, (jb + j + 1) * (tn // shift), 0)),
        ],
        out_specs=pl.BlockSpec((TM_MM, tn), lambda j, i: (i, j)),
        scratch_shapes=[pltpu.VMEM((tn, k), BF16)],
        compiler_params=_cparams(("arbitrary", "arbitrary")),
        name="proj_in_shift",
    )(x, w_t, w_t)


def _mm3_kernel(a_ref, b_ref, c_ref, w_ref, o_ref, wbf_ref):
    @pl.when(pl.program_id(1) == 0)
    def _():
        wbf_ref[...] = w_ref[...].astype(BF16)

    ka, kb = a_ref.shape[1], b_ref.shape[1]
    acc = _dot(a_ref[...], wbf_ref[0:ka, :])
    acc += _dot(b_ref[...], wbf_ref[ka:ka + kb, :])
    acc += _dot(c_ref[...], wbf_ref[ka + kb:, :])
    o_ref[...] = acc


def _mm_out(ya, yb, yc, w_full, layer):
    t = ya.shape[0]
    k, n = w_full.shape[1], w_full.shape[2]
    return pl.pallas_call(
        _mm3_kernel,
        out_shape=jax.ShapeDtypeStruct((t, n), F32),
        grid=(n // TN_MM, t // TM_MM),
        in_specs=[
            pl.BlockSpec((TM_MM, ya.shape[1]), lambda j, i: (i, 0)),
            pl.BlockSpec((TM_MM, yb.shape[1]), lambda j, i: (i, 0)),
            pl.BlockSpec((TM_MM, yc.shape[1]), lambda j, i: (i, 0)),
            pl.BlockSpec((None, k, TN_MM), lambda j, i: (layer, 0, j)),
        ],
        out_specs=pl.BlockSpec((TM_MM, TN_MM), lambda j, i: (i, j)),
        scratch_shapes=[pltpu.VMEM((k, TN_MM), BF16)],
        compiler_params=_cparams(("arbitrary", "arbitrary")),
        name="proj_out",
    )(ya, yb, yc, w_full)


def _ssm_params(lam_re, lam_im, log_step, b_re, b_im, c_re, c_im):
    step = jnp.exp(log_step)[:, None]
    ar, ai = lam_re * step, lam_im * step
    mag = jnp.exp(ar)
    lbr, lbi = mag * jnp.cos(ai), mag * jnp.sin(ai)
    den = lam_re * lam_re + lam_im * lam_im
    cfr = ((lbr - 1.0) * lam_re + lbi * lam_im) / den
    cfi = (lbi * lam_re - (lbr - 1.0) * lam_im) / den
    bbr = cfr[..., None] * b_re - cfi[..., None] * b_im
    bbi = cfr[..., None] * b_im + cfi[..., None] * b_re
    nj = SSM_GROUPS // SSM_GB
    eye = jnp.eye(SSM_GB, dtype=F32)

    def bmat(bb):
        bb = bb.reshape(nj, SSM_GB, SSM_STATE, SSM_GROUP)
        m = jnp.einsum("ab,jbph->jahbp", eye, bb)
        return m.reshape(nj, SSM_GB * SSM_GROUP, SSM_GB * SSM_STATE)

    def cmat(cc):
        cc = cc.reshape(nj, SSM_GB, SSM_GROUP, SSM_STATE)
        m = jnp.einsum("ab,jbhp->japbh", eye, cc)
        return m.reshape(nj, SSM_GB * SSM_STATE, SSM_GB * SSM_GROUP)

    b_mat = jnp.concatenate([bmat(bbr), bmat(bbi)], axis=-1).astype(BF16)
    c_mat = jnp.concatenate([cmat(c_re), -cmat(c_im)], axis=1).astype(BF16)

    def power(kk):
        m = jnp.exp(kk * ar[None])
        return ((m * jnp.cos(kk * ai[None])).reshape(-1, SSM_LANES),
                (m * jnp.sin(kk * ai[None])).reshape(-1, SSM_LANES))

    rows = jnp.arange(SUBLANES, dtype=F32)[:, None]
    shifts = []
    for s in (1, 2, 4):
        pr, pi = power(jnp.full((1, 1, 1), float(s), F32))
        keep = rows >= s
        shifts += [jnp.where(keep, pr, 0.0), jnp.where(keep, pi, 0.0)]
    pr, pi = power(jnp.arange(1, SUBLANES + 1, dtype=F32)[:, None, None])
    lam_tab = jnp.stack(shifts + [pr, pi])
    return b_mat, c_mat, lam_tab


def _ssm_kernel(u_ref, bm_ref, cm_ref, lt_ref, d_ref, gw_ref, gb_ref, o_ref, xr_ref, xi_ref, st_ref):
    tb = u_ref.shape[0]
    nj = bm_ref.shape[0]
    gw = SSM_GB * SSM_GROUP
    sw = SSM_GB * SSM_STATE

    @pl.when(pl.program_id(1) == 0)
    def _():
        st_ref[...] = jnp.zeros_like(st_ref)

    u = u_ref[...]
    ub = u.astype(BF16)
    for j in range(nj):
        bu = _dot(ub[:, j * gw:(j + 1) * gw], bm_ref[j])
        xr_ref[:, j * sw:(j + 1) * sw] = bu[:, :sw]
        xi_ref[:, j * sw:(j + 1) * sw] = bu[:, sw:]

    def row_body(r, carry):
        rs = pl.ds(pl.multiple_of(r * SUBLANES, SUBLANES), SUBLANES)
        for c in range(SSM_LANES // SSM_LC):
            sl = slice(c * SSM_LC, (c + 1) * SSM_LC)
            xr = xr_ref[rs, sl]
            xi = xi_ref[rs, sl]
            for k, s in enumerate((1, 2, 4)):
                lr, li = lt_ref[2 * k, :, sl], lt_ref[2 * k + 1, :, sl]
                sr = pltpu.roll(xr, s, axis=0)
                si = pltpu.roll(xi, s, axis=0)
                xr, xi = xr + (lr * sr - li * si), xi + (lr * si + li * sr)
            pr, pi = lt_ref[6, :, sl], lt_ref[7, :, sl]
            cr, ci = st_ref[0:1, sl], st_ref[1:2, sl]
            xr, xi = xr + (pr * cr - pi * ci), xi + (pr * ci + pi * cr)
            xr_ref[rs, sl] = xr
            xi_ref[rs, sl] = xi
            st_ref[0:1, sl] = xr[SUBLANES - 1:, :]
            st_ref[1:2, sl] = xi[SUBLANES - 1:, :]
        return carry

    lax.fori_loop(0, tb // SUBLANES, row_body, 0)

    ys = []
    for j in range(nj):
        xs = jnp.concatenate([xr_ref[:, j * sw:(j + 1) * sw], xi_ref[:, j * sw:(j + 1) * sw]], axis=-1)
        ys.append(_dot(xs.astype(BF16), cm_ref[j]))
    y = jnp.concatenate(ys, axis=-1) + d_ref[...] * u
    y = jax.nn.gelu(y)
    gate = _dot(y.astype(BF16), gw_ref[...]) + gb_ref[...]
    o_ref[...] = (y * jax.nn.sigmoid(gate)).astype(o_ref.dtype)


def _ssm(u, b_mat, c_mat, lam_tab, d_skip, glu_w, glu_b, nbatch):
    t, w = u.shape
    per_b = t // nbatch // TB_SSM
    full = lambda a: pl.BlockSpec(a.shape, lambda b, i: (0,) * a.ndim)
    d_row = d_skip.reshape(1, w)
    gb_row = glu_b.reshape(1, w)
    return pl.pallas_call(
        _ssm_kernel,
        out_shape=jax.ShapeDtypeStruct((t, w), BF16),
        grid=(nbatch, per_b),
        in_specs=[
            pl.BlockSpec((TB_SSM, w), lambda b, i: (b * per_b + i, 0)),
            full(b_mat), full(c_mat), full(lam_tab), full(d_row), full(glu_w), full(gb_row),
        ],
        out_specs=pl.BlockSpec((TB_SSM, w), lambda b, i: (b * per_b + i, 0)),
        scratch_shapes=[
            pltpu.VMEM((TB_SSM, SSM_LANES), F32),
            pltpu.VMEM((TB_SSM, SSM_LANES), F32),
            pltpu.VMEM((2, SSM_LANES), F32),
        ],
        compiler_params=_cparams(("arbitrary", "arbitrary")),
        name="s5_mixer",
    )(u, b_mat, c_mat, lam_tab, d_row, glu_w, gb_row)


def _log_sigmoid(z):
    return jnp.minimum(z, 0.0) - jnp.log1p(jnp.exp(-jnp.abs(z)))


def _silu(x):
    return x * jax.nn.sigmoid(x)


def _gla_kernel(qk_ref, v_ref, g_ref, a_ref, wa_ref, ba_ref, gain_ref, o_ref, st_ref):
    tb = qk_ref.shape[0]
    ck = GLA_CHUNK

    @pl.when(pl.program_id(1) == 0)
    def _():
        st_ref[...] = jnp.zeros_like(st_ref)

    log_a = _log_sigmoid(_dot3(a_ref[:, :GLA_RANK], wa_ref[...]) + ba_ref[...]) * (1.0 / GLA_TAU)
    ri = lax.broadcasted_iota(I32, (ck, ck), 0)
    ci = lax.broadcasted_iota(I32, (ck, ck), 1)
    causal = ri >= ci
    tri = jnp.where(causal, 1.0, 0.0).astype(BF16)
    for c in range(tb // ck):
        rows = slice(c * ck, (c + 1) * ck)
        la_hi, la_lo = _split_bf16(log_a[rows])
        dec = _dot(tri, la_hi) + _dot(tri, la_lo)
        for h in range(GLA_HEADS):
            kcol = slice(h * GLA_DK, (h + 1) * GLA_DK)
            vcol = slice(h * GLA_DV, (h + 1) * GLA_DV)
            b = dec[:, kcol]
            b_last = b[ck - 1:ck, :]
            q = qk_ref[rows, kcol] * (GLA_DK ** -0.5)
            k = qk_ref[rows, GLA_QK + h * GLA_DK:GLA_QK + (h + 1) * GLA_DK]
            v = v_ref[rows, vcol].astype(BF16)
            q_dec = (q * jnp.exp(b)).astype(BF16)
            k_intra = (k * jnp.exp(-b)).astype(BF16)
            k_state = (k * jnp.exp(b_last - b)).astype(BF16)
            scores = jnp.where(causal, _dot_nt(q_dec, k_intra), 0.0)
            st = st_ref[h]
            o = _dot(scores.astype(BF16), v) + _dot_nt(q_dec, st.astype(BF16))
            st_ref[h] = st * jnp.exp(b_last) + _dot_tn(v, k_state)
            o = _rms(o) * gain_ref[:, vcol] * _silu(g_ref[rows, vcol])
            o_ref[rows, vcol] = o.astype(o_ref.dtype)


def _gla(gqk, gv, gg, ga, wa, ba, gain, nbatch):
    t = gqk.shape[0]
    per_b = t // nbatch // TB_GLA
    row = lambda w: pl.BlockSpec((TB_GLA, w), lambda b, i: (b * per_b + i, 0))
    full = lambda a: pl.BlockSpec(a.shape, lambda b, i: (0,) * a.ndim)
    ba_row = ba.reshape(1, GLA_QK)
    gain_row = gain.reshape(1, GLA_WIDTH)
    return pl.pallas_call(
        _gla_kernel,
        out_shape=jax.ShapeDtypeStruct((t, GLA_WIDTH), BF16),
        grid=(nbatch, per_b),
        in_specs=[row(2 * GLA_QK), row(GLA_WIDTH), row(GLA_WIDTH), row(LANES),
                  full(wa), full(ba_row), full(gain_row)],
        out_specs=row(GLA_WIDTH),
        scratch_shapes=[pltpu.VMEM((GLA_HEADS, GLA_DV, GLA_DK), F32)],
        compiler_params=_cparams(("arbitrary", "arbitrary")),
        name="gla_mixer",
    )(gqk, gv, gg, ga, wa, ba_row, gain_row)


def _ret_kernel(qk_ref, v_ref, g_ref, pos_ref, inv_ref, gain_ref, o_ref, st_ref):
    tb = qk_ref.shape[0]

    @pl.when(pl.program_id(1) == 0)
    def _():
        st_ref[...] = jnp.zeros_like(st_ref)

    ang = pos_ref[...].astype(F32) * inv_ref[...]
    lane = lax.broadcasted_iota(I32, (1, RET_DK), 1)
    cos = jnp.cos(ang)
    sin = jnp.sin(ang) * jnp.where(lane < RET_DK // 2, -1.0, 1.0)
    ri = lax.broadcasted_iota(I32, (tb, tb), 0)
    ci = lax.broadcasted_iota(I32, (tb, tb), 1)
    rel = (ri - ci).astype(F32)
    idx = lax.broadcasted_iota(I32, (tb, 1), 0).astype(F32)
    for h in range(RET_HEADS):
        log_gamma = math.log1p(-(2.0 ** (-5.0 - h)))
        kcol = slice(h * RET_DK, (h + 1) * RET_DK)
        vcol = slice(h * RET_DV, (h + 1) * RET_DV)
        q = qk_ref[:, kcol]
        k = qk_ref[:, RET_QK + h * RET_DK:RET_QK + (h + 1) * RET_DK]
        q = (q * cos + pltpu.roll(q, RET_DK // 2, axis=1) * sin) * (RET_DK ** -0.5)
        k = k * cos + pltpu.roll(k, RET_DK // 2, axis=1) * sin
        v = v_ref[:, vcol].astype(BF16)
        decay = jnp.where(rel >= 0, jnp.exp(jnp.maximum(rel, 0.0) * log_gamma), 0.0)
        scores = _dot_nt(q.astype(BF16), k.astype(BF16)) * decay
        q_w = jnp.exp((idx + 1.0) * log_gamma)
        k_w = jnp.exp((tb - 1.0 - idx) * log_gamma)
        st = st_ref[h]
        o = _dot(scores.astype(BF16), v) + _dot((q * q_w).astype(BF16), st.astype(BF16))
        st_ref[h] = st * math.exp(tb * log_gamma) + _dot_tn((k * k_w).astype(BF16), v)
        o = _rms(o) * gain_ref[:, vcol] * _silu(g_ref[:, vcol])
        o_ref[:, vcol] = o.astype(o_ref.dtype)


def _ret(rqk, rv, rg, pos, inv, gain, nbatch):
    t = rqk.shape[0]
    per_b = t // nbatch // TB_RET
    row = lambda w: pl.BlockSpec((TB_RET, w), lambda b, i: (b * per_b + i, 0))
    full = lambda a: pl.BlockSpec(a.shape, lambda b, i: (0,) * a.ndim)
    gain_row = gain.reshape(1, RET_WIDTH)
    return pl.pallas_call(
        _ret_kernel,
        out_shape=jax.ShapeDtypeStruct((t, RET_WIDTH), BF16),
        grid=(nbatch, per_b),
        in_specs=[row(2 * RET_QK), row(RET_WIDTH), row(RET_WIDTH), row(1), full(inv), full(gain_row)],
        out_specs=row(RET_WIDTH),
        scratch_shapes=[pltpu.VMEM((RET_HEADS, RET_DK, RET_DV), F32)],
        compiler_params=_cparams(("arbitrary", "arbitrary")),
        name="ret_mixer",
    )(rqk, rv, rg, pos, inv, gain_row)


def _mix_post_kernel(y_ref, x_ref, mod_ref, gpost_ref, gpre_ref, rwh_ref, rwl_ref, rb_ref,
                     xo_ref, ho_ref, ti_ref, tw_ref):
    m = mod_ref[...]
    x = x_ref[...] + m[2:3] * (_rms(y_ref[...]) * gpost_ref[...])
    xo_ref[...] = x
    h = _rms(x) * gpre_ref[...] * (1.0 + m[4:5]) + m[3:4]
    ho_ref[...] = h
    hh, hl = _split_bf16(h)
    logits = _dot(hh, rwh_ref[...]) + _dot(hl, rwh_ref[...]) + _dot(hh, rwl_ref[...]) + rb_ref[...]
    lane = lax.broadcasted_iota(I32, logits.shape, 1)
    vals, idxs = [], []
    for _ in range(TOP_K):
        mx = jnp.max(logits, axis=-1, keepdims=True)
        ix = jnp.min(jnp.where(logits == mx, lane, LANES), axis=-1, keepdims=True)
        vals.append(mx)
        idxs.append(ix)
        logits = jnp.where(lane == ix, -jnp.inf, logits)
    es = [jnp.exp(v - vals[0]) for v in vals]
    inv_sum = 1.0 / (es[0] + es[1] + es[2] + es[3])
    ti = jnp.zeros(lane.shape, I32)
    tw = jnp.zeros(lane.shape, F32)
    for kk in range(TOP_K):
        ti = jnp.where(lane == kk, idxs[kk], ti)
        tw = jnp.where(lane == kk, es[kk] * inv_sum, tw)
    ti_ref[...] = ti
    tw_ref[...] = tw


def _mix_post(y, x2, mod6, gpost, gpre, rw_hi, rw_lo, rb, seq):
    t, d = x2.shape
    per_b = seq // TM_ROW
    row = lambda w: pl.BlockSpec((TM_ROW, w), lambda i: (i, 0))
    full = lambda a: pl.BlockSpec(a.shape, lambda i: (0,) * a.ndim)
    gpost, gpre = gpost.reshape(1, d), gpre.reshape(1, d)
    return pl.pallas_call(
        _mix_post_kernel,
        out_shape=(jax.ShapeDtypeStruct((t, d), F32), jax.ShapeDtypeStruct((t, d), F32),
                   jax.ShapeDtypeStruct((t, LANES), I32), jax.ShapeDtypeStruct((t, LANES), F32)),
        grid=(t // TM_ROW,),
        in_specs=[row(d), row(d), pl.BlockSpec((None, N_MOD, d), lambda i: (i // per_b, 0, 0)),
                  full(gpost), full(gpre), full(rw_hi), full(rw_lo), full(rb)],
        out_specs=(row(d), row(d), row(LANES), row(LANES)),
        compiler_params=_cparams(("arbitrary",)),
        name="mix_post_router",
    )(y, x2, mod6, gpost, gpre, rw_hi, rw_lo, rb)


def _route(top_idx, n_tiles):
    t = top_idx.shape[0]
    assert math.gcd(ROUTE_STRIDE, t) == 1
    perm = (jnp.arange(t, dtype=I32) * ROUTE_STRIDE) % t
    inv_perm = (jnp.arange(t, dtype=I32) * pow(ROUTE_STRIDE, -1, t)) % t
    e_flat = top_idx[perm].reshape(-1)
    onehot =e_flat[:, None] == jnp.arange(N_EXPERTS, dtype=I32)[None, :]
    blk = 256
    nblk = (t * TOP_K) // blk
    oh3 = onehot.astype(BF16).reshape(nblk, blk, N_EXPERTS)
    within = jnp.einsum("ij,bjk->bik", jnp.tril(jnp.ones((blk, blk), BF16)), oh3, preferred_element_type=F32)
    tot = within[:, -1, :]
    before = jnp.dot(jnp.tril(jnp.ones((nblk, nblk), BF16), -1), tot.astype(BF16), preferred_element_type=F32)
    csum = (within + before[:, None, :]).reshape(t * TOP_K, N_EXPERTS)
    counts = (before[-1] + tot[-1]).astype(I32)
    padded = ((counts + TM_MOE - 1) // TM_MOE) * TM_MOE
    ends = jnp.cumsum(padded)
    starts = ends - padded
    dest = jnp.sum(jnp.where(onehot, csum - 1.0 + starts.astype(F32)[None, :], 0.0), axis=1).astype(I32)
    row_src = (jnp.arange(n_tiles * TM_MOE, dtype=I32) % t).at[dest].set(jnp.repeat(perm, TOP_K))
    n_used = ends[-1] // TM_MOE
    tile_ids = jnp.minimum(jnp.arange(n_tiles, dtype=I32), n_used - 1)
    tile_expert = jnp.sum(ends[None, :] <= (tile_ids * TM_MOE)[:, None], axis=1).astype(I32)
    pos_t = dest.reshape(t, TOP_K)[inv_perm].T.reshape(-1)
    return row_src, tile_expert, n_used.reshape(1).astype(I32), pos_t


def _new_weights(te_ref, m):
    prev = te_ref[jnp.maximum(m - 1, 0)]
    return jnp.logical_or(m == 0, te_ref[m] != prev)


def _gu_kernel(te_ref, nu_ref, src_ref, h_ref, wg_ref, wu_ref, bg_ref, bu_ref, o_ref,
               wcat_ref, xa_ref, xb_ref, sem, *, unrolled_issue):
    m = pl.program_id(1)
    nt = pl.num_programs(1)
    step = pl.program_id(0) * nt + m
    n_steps = pl.num_programs(0) * nt
    next_tile = jnp.where(m + 1 < nt, m + 1, 0)
    bufs = (xa_ref, xb_ref)

    def row_copy(src_row, slot, r):
        return pltpu.make_async_copy(h_ref.at[pl.ds(src_row, 1)], bufs[slot].at[pl.ds(r, 1)], sem.at[slot])

    def issue(tile, slot, unrolled):
        base = tile * TM_MOE
        if unrolled:
            for r in range(TM_MOE):
                row_copy(src_ref[base + r], slot, r).start()
        else:
            def body(r, carry):
                row_copy(src_ref[base + r], slot, r).start()
                return carry
            lax.fori_loop(0, TM_MOE, body, 0, unroll=8)

    def wait(slot):
        pltpu.make_async_copy(h_ref.at[pl.ds(0, TM_MOE)], bufs[slot], sem.at[slot]).wait()

    @pl.when(step == 0)
    def _():
        issue(0, 0, False)

    @pl.when(_new_weights(te_ref, m))
    def _():
        wcat_ref[:, 0:TN_GU] = wg_ref[...].astype(BF16)
        wcat_ref[:, TN_GU:] = wu_ref[...].astype(BF16)

    valid = m < nu_ref[0]
    for slot in range(2):
        mine = step % 2 == slot

        @pl.when(jnp.logical_and(mine, valid))
        def _(slot=slot):
            wait(slot)
            issue(next_tile, 1 - slot, unrolled_issue)
            x = bufs[slot][...].astype(BF16)
            gu = _dot(x, wcat_ref[...])
            gate = gu[:, 0:TN_GU] + bg_ref[...]
            up = gu[:, TN_GU:] + bu_ref[...]
            gate = jnp.minimum(gate, SWIGLU_LIMIT)
            up = jnp.clip(up, -SWIGLU_LIMIT, SWIGLU_LIMIT)
            o_ref[...] = ((up + 1.0) * gate * jax.nn.sigmoid(SWIGLU_ALPHA * gate)).astype(o_ref.dtype)

        @pl.when(jnp.logical_and(mine, jnp.logical_not(valid)))
        def _(slot=slot):
            wait(slot)
            issue(next_tile, 1 - slot, False)
            o_ref[...] = jnp.zeros_like(o_ref)

    for slot in range(2):
        @pl.when(jnp.logical_and(step == n_steps - 1, step % 2 == 1 - slot))
        def _(slot=slot):
            wait(slot)


def _expert_gu(tile_expert, n_used, row_src, h, w_gu, b_gu4, layer, *, unrolled_issue=True):
    mrows = row_src.shape[0]
    nt = mrows // TM_MOE
    nn = D_EXPERT // TN_GU
    wspec = lambda off: pl.BlockSpec((None, None, D_MODEL, TN_GU),
                                     lambda n, m, te, nu, src: (layer, te[m], 0, off + n))
    bspec = lambda off: pl.BlockSpec((None, None, 1, TN_GU), lambda n, m, te, nu, src: (layer, te[m], 0, off + n))
    return pl.pallas_call(
        functools.partial(_gu_kernel, unrolled_issue=unrolled_issue),
        out_shape=jax.ShapeDtypeStruct((mrows, D_EXPERT), BF16),
        grid_spec=pltpu.PrefetchScalarGridSpec(
            num_scalar_prefetch=3,
            grid=(nn, nt),
            in_specs=[pl.BlockSpec(memory_space=pl.ANY), wspec(0), wspec(nn), bspec(0), bspec(nn)],
            out_specs=pl.BlockSpec((TM_MOE, TN_GU), lambda n, m, te, nu, src: (m, n)),
            scratch_shapes=[pltpu.VMEM((D_MODEL, 2 * TN_GU), BF16),
                            pltpu.VMEM((TM_MOE, D_MODEL), F32), pltpu.VMEM((TM_MOE, D_MODEL), F32),
                            pltpu.SemaphoreType.DMA((2,))],
        ),
        compiler_params=_cparams(("arbitrary", "arbitrary")),
        name="moe_gate_up",
    )(tile_expert, n_used, row_src, h, w_gu, w_gu, b_gu4, b_gu4)


def _down_kernel(te_ref, nu_ref, a_ref, w_ref, b_ref, o_ref, wb_ref):
    m = pl.program_id(1)

    @pl.when(_new_weights(te_ref, m))
    def _():
        wb_ref[...] = w_ref[...].astype(BF16)

    @pl.when(m < nu_ref[0])
    def _():
        o_ref[...] = _dot(a_ref[...], wb_ref[...]) + b_ref[...]

    @pl.when(m >= nu_ref[0])
    def _():
        o_ref[...] = jnp.zeros_like(o_ref)


def _expert_down(tile_expert, n_used, act, w_down, b_down4, layer):
    mrows = act.shape[0]
    nt = mrows // TM_MOE
    return pl.pallas_call(
        _down_kernel,
        out_shape=jax.ShapeDtypeStruct((mrows, D_MODEL), F32),
        grid_spec=pltpu.PrefetchScalarGridSpec(
            num_scalar_prefetch=2,
            grid=(D_MODEL // TN_DOWN, nt),
            in_specs=[
                pl.BlockSpec((TM_MOE, D_EXPERT), lambda n, m, te, nu: (jnp.minimum(m, nu[0] - 1), 0)),
                pl.BlockSpec((None, None, D_EXPERT, TN_DOWN), lambda n, m, te, nu: (layer, te[m], 0, n)),
                pl.BlockSpec((None, None, 1, TN_DOWN), lambda n, m, te, nu: (layer, te[m], 0, n)),
            ],
            out_specs=pl.BlockSpec((TM_MOE, TN_DOWN), lambda n, m, te, nu: (m, n)),
            scratch_shapes=[pltpu.VMEM((D_EXPERT, TN_DOWN), BF16)],
        ),
        compiler_params=_cparams(("arbitrary", "arbitrary")),
        name="moe_down",
    )(tile_expert, n_used, act, w_down, b_down4)


def _combine_kernel(pos_ref, ys_ref, tw_ref, x_ref, mod_ref, gain_ref, o_ref, buf_ref, sem):
    i = pl.program_id(0)
    n = pl.num_programs(0)
    tm = x_ref.shape[0]
    t_total = n * tm
    rows = TOP_K * tm

    def issue(tile, slot, unrolled):
        for kk in range(TOP_K):
            base = kk * t_total + tile * tm

            def start(r, priority=0, kk=kk, base=base):
                pltpu.make_async_copy(ys_ref.at[pl.ds(pos_ref[base + r], 1)],
                                      buf_ref.at[slot, pl.ds(kk * tm + r, 1)], sem.at[slot]).start(priority)

            if unrolled:
                for r in range(tm):
                    start(r, r % 2)
            else:
                def body(r, carry, start=start):
                    start(r)
                    return carry
                lax.fori_loop(0, tm, body, 0, unroll=8)

    @pl.when(i == 0)
    def _():
        issue(0, 0, False)

    for slot in range(2):
        @pl.when(i % 2 == slot)
        def _(slot=slot):
            @pl.when(i + 1 < n)
            def _():
                issue(i + 1, 1 - slot, True)

            pltpu.make_async_copy(ys_ref.at[pl.ds(0, rows)], buf_ref.at[slot], sem.at[slot]).wait()
            tw = tw_ref[...]
            y = tw[:, 0:1] * buf_ref[slot, 0:tm, :]
            for kk in range(1, TOP_K):
                y += tw[:, kk:kk + 1] * buf_ref[slot, kk * tm:(kk + 1) * tm, :]
            m = mod_ref[...]
            o_ref[...] = x_ref[...] + m[5:6] * (_rms(y) * gain_ref[...])


def _combine(pos_t, ys, top_w, x2, mod6, gain, seq):
    t, d = x2.shape
    per_b = seq // TM_CMB
    gain = gain.reshape(1, d)
    return pl.pallas_call(
        _combine_kernel,
        out_shape=jax.ShapeDtypeStruct((t, d), F32),
        grid_spec=pltpu.PrefetchScalarGridSpec(
            num_scalar_prefetch=1,
            grid=(t // TM_CMB,),
            in_specs=[
                pl.BlockSpec(memory_space=pl.ANY),
                pl.BlockSpec((TM_CMB, LANES), lambda i, pos: (i, 0)),
                pl.BlockSpec((TM_CMB, d), lambda i, pos: (i, 0)),
                pl.BlockSpec((None, N_MOD, d), lambda i, pos: (i // per_b, 0, 0)),
                pl.BlockSpec((1, d), lambda i, pos: (0, 0)),
            ],
            out_specs=pl.BlockSpec((TM_CMB, d), lambda i, pos: (i, 0)),
            scratch_shapes=[pltpu.VMEM((2, TOP_K * TM_CMB, d), F32), pltpu.SemaphoreType.DMA((2,))],
        ),
        compiler_params=_cparams(("arbitrary",)),
        name="moe_combine",
    )(pos_t, ys, top_w, x2, mod6, gain)


def kernel(x, c, positions, ada_w, ada_b, mix_pre_gain, mix_post_gain, ffn_pre_gain, ffn_post_gain,
           w_in, w_out, ssm_lam_re, ssm_lam_im, ssm_log_step, ssm_b_re, ssm_b_im, ssm_c_re, ssm_c_im,
           ssm_d, ssm_glu_w, ssm_glu_b, gla_wa, gla_ba, gla_norm_gain, ret_norm_gain,
           router_w, router_b, exp_w_gu, exp_b_gu, exp_w_down, exp_b_down):
    nbatch, seq, d = x.shape
    t = nbatch * seq
    depth = ada_w.shape[0]
    n_tiles = (t * TOP_K) // TM_MOE + N_EXPERTS

    x2 = x.reshape(t, d)
    pos = positions.reshape(t, 1)
    half = RET_DK // 2
    inv = jnp.power(ROPE_BASE, -jnp.arange(half, dtype=F32) / half)
    inv = jnp.concatenate([inv, inv]).reshape(1, RET_DK)
    mod = _ada(c, ada_w, ada_b)

    rw =jnp.pad(router_w, ((0, 0), (0, 0), (0, LANES - N_EXPERTS)))
    rw_hi = rw.astype(BF16)
    rw_lo = (rw - rw_hi.astype(F32)).astype(BF16)
    rb = jnp.pad(router_b, ((0, 0), (0, LANES - N_EXPERTS)), constant_values=-1e30)
    b_gu4 = exp_b_gu.reshape(depth, N_EXPERTS, 1, 2 * D_EXPERT)
    b_down4 = exp_b_down.reshape(depth, N_EXPERTS, 1, d)
    glu_w = ssm_glu_w.astype(BF16)
    w_in_t = jnp.swapaxes(w_in, 1, 2)

    for l in range(depth):
        mod6 = mod[l].reshape(nbatch, N_MOD, d)
        h = _prenorm(x2, mix_pre_gain[l], mod6, seq, shift_i=0, scale_i=1)
        u = _mm(h, w_in_t, l, OFF_U, SSM_WIDTH)
        gqk = _mm(h, w_in_t, l, OFF_GQK, 2 * GLA_QK)
        gv = _mm(h, w_in_t, l, OFF_GV, GLA_WIDTH)
        gg = _mm(h, w_in_t, l, OFF_GG, GLA_WIDTH)
        ga = _mm(h, w_in_t, l, OFF_GA, LANES, tn=LANES)
        rqk = _mm_shift(h, w_in_t, l, OFF_TAIL, 2 * RET_QK)
        rv = _mm_shift(h, w_in_t, l, OFF_TAIL + 2 * RET_QK, RET_WIDTH)
        rg = _mm_shift(h, w_in_t, l, OFF_TAIL + 2 * RET_QK + RET_WIDTH, RET_WIDTH)

        b_mat, c_mat, lam_tab = _ssm_params(ssm_lam_re[l], ssm_lam_im[l], ssm_log_step[l], ssm_b_re[l],
                                            ssm_b_im[l], ssm_c_re[l], ssm_c_im[l])
        y_ssm = _ssm(u, b_mat, c_mat, lam_tab, ssm_d[l], glu_w[l], ssm_glu_b[l], nbatch)
        y_gla = _gla(gqk, gv, gg, ga, gla_wa[l], gla_ba[l], gla_norm_gain[l], nbatch)
        y_ret = _ret(rqk, rv, rg, pos, inv, ret_norm_gain[l], nbatch)
        y = _mm_out(y_ssm, y_gla, y_ret, w_out, l)

        x2, h2, top_idx, top_w = _mix_post(y, x2, mod6, mix_post_gain[l], ffn_pre_gain[l],
                                           rw_hi[l], rw_lo[l], rb[l:l + 1], seq)
        row_src, tile_expert, n_used, pos_t = _route(top_idx[:, :TOP_K], n_tiles)
        act = _expert_gu(tile_expert, n_used, row_src, h2, exp_w_gu, b_gu4, l)
        ys = _expert_down(tile_expert, n_used, act, exp_w_down, b_down4, l)
        x2 = _combine(pos_t, ys, top_w, x2, mod6, ffn_post_gain[l], seq)
    return x2.reshape(nbatch, seq, d)
```
